```python
import math
import jax, jax.numpy as jnp
from jax import lax
import numpy as np

D_MODEL = 1024
BATCH = 4
SEQ = 4096
DEPTH = 4
DEC_BATCH = 128
DEC_SEQ = 4
PAST_LEN = 2048
PAGE_SIZE = 128

N_HEADS = 8
HEAD_DIM = 64
ATTN_W = N_HEADS * HEAD_DIM
CONV_CH = 512
CONV_W = 31
D_FF = 2816
MOBA_BLOCK = 256
MOBA_TOPK = 3
N_BUCKETS = 32
MAX_DISTANCE = 128
QUERY_BLOCK = 128
N_MOD = 9
HALF = 0.5
EPS = 1e-6
NEG = -1e30
IN_SPLITS = [CONV_CH, 2 * CONV_CH, 2 * CONV_CH + ATTN_W, 2 * CONV_CH + 2 * ATTN_W,
             2 * CONV_CH + 3 * ATTN_W, 2 * CONV_CH + 3 * ATTN_W + D_MODEL]
IN_COLS = 2 * CONV_CH + 3 * ATTN_W + 2 * D_MODEL

kernel_name = 'adaln_conformer_moba_gated_decoder_step'


def rmsnorm(x, g):
    xf = x.astype(jnp.float32)
    y = xf * lax.rsqrt(jnp.mean(xf * xf, axis=-1, keepdims=True) + EPS)
    return (y * g.astype(jnp.float32)).astype(x.dtype)


def layernorm(x, g, b):
    xf = x.astype(jnp.float32)
    mu = jnp.mean(xf, axis=-1, keepdims=True)
    xc = xf - mu
    y = xc * lax.rsqrt(jnp.mean(xc * xc, axis=-1, keepdims=True) + EPS)
    return (y * g.astype(jnp.float32) + b.astype(jnp.float32)).astype(x.dtype)


def modulate(x, g, shift, scale):
    return rmsnorm(x, g) * (1 + scale) + shift


def swiglu(h, wg, wu, wd):
    return (jax.nn.silu(h @ wg) * (h @ wu)) @ wd


def t5_bucket(dist):
    n = jnp.maximum(dist, 0)
    max_exact = N_BUCKETS // 2
    nf = jnp.maximum(n, 1).astype(jnp.float32)
    large = max_exact + (jnp.log(nf / max_exact) / math.log(MAX_DISTANCE / max_exact)
                         * (N_BUCKETS - max_exact)).astype(jnp.int32)
    large = jnp.minimum(large, N_BUCKETS - 1)
    return jnp.where(n < max_exact, n, large)


def causal_dwconv(u_ext, w):
    return lax.conv_general_dilated(u_ext, w[:, None, :], (1,), 'VALID',
                                    dimension_numbers=('NWC', 'WIO', 'NWC'),
                                    feature_group_count=CONV_CH)


def moba_blocks(k):
    L = k.shape[0]
    nb = -(-L // MOBA_BLOCK)
    k = jnp.pad(k, ((0, nb * MOBA_BLOCK - L), (0, 0), (0, 0)))
    return k.reshape(nb, MOBA_BLOCK, N_HEADS, HEAD_DIM).transpose(2, 0, 1, 3)


def moba_query(q, qpos, kb, vb, kmean, rel_bias):
    nq = q.shape[0]
    nb = kb.shape[1]
    kk = min(MOBA_TOPK, nb)
    scale = HEAD_DIM ** -0.5
    own = qpos // MOBA_BLOCK
    hidx = jnp.arange(N_HEADS)
    gate = jnp.einsum('qhd,hnd->qhn', q.astype(jnp.float32), kmean)
    fully_past = jnp.arange(nb)[None, None, :] < own[:, None, None]
    gate = jnp.where(fully_past, gate, NEG)
    _, sel = lax.top_k(gate, kk)
    valid = sel < own[:, None, None]
    kg = kb[hidx[None, :, None], sel]
    vg = vb[hidx[None, :, None], sel]
    ko = kb[hidx[None, :], own[:, None]]
    vo = vb[hidx[None, :], own[:, None]]
    offs = jnp.arange(MOBA_BLOCK, dtype=jnp.int32)
    kpos_g = sel[..., None] * MOBA_BLOCK + offs
    kpos_o = own[:, None] * MOBA_BLOCK + offs
    rel = rel_bias.astype(jnp.float32)
    bias_g = rel[hidx[None, :, None, None], t5_bucket(qpos[:, None, None, None] - kpos_g)]
    bias_o = rel[hidx[None, :, None], t5_bucket(qpos[:, None, None] - kpos_o[:, None, :])]
    lg = jnp.einsum('qhd,qhskd->qhsk', q, kg, preferred_element_type=jnp.float32) * scale + bias_g
    lg = jnp.where(valid[..., None], lg, NEG)
    lo = jnp.einsum('qhd,qhkd->qhk', q, ko, preferred_element_type=jnp.float32) * scale + bias_o
    lo = jnp.where((kpos_o <= qpos[:, None])[:, None, :], lo, NEG)
    p = jax.nn.softmax(jnp.concatenate([lg.reshape(nq, N_HEADS, kk * MOBA_BLOCK), lo], axis=-1), axis=-1)
    pg = p[..., :kk * MOBA_BLOCK].reshape(nq, N_HEADS, kk, MOBA_BLOCK).astype(vg.dtype)
    po = p[..., kk * MOBA_BLOCK:].astype(vo.dtype)
    out = jnp.einsum('qhsk,qhskd->qhd', pg, vg) + jnp.einsum('qhk,qhkd->qhd', po, vo)
    return out.astype(q.dtype)


def moba_prompt(q, k, v, rel_bias):
    s = q.shape[1]
    nch = s // QUERY_BLOCK
    qpos = jnp.arange(s, dtype=jnp.int32).reshape(nch, QUERY_BLOCK)

    def per_seq(args):
        qs, ks, vs = args
        kb = moba_blocks(ks)
        vb = moba_blocks(vs)
        kmean = jnp.mean(kb.astype(jnp.float32), axis=2)
        qc = qs.reshape(nch, QUERY_BLOCK, N_HEADS, HEAD_DIM)
        oc = lax.map(lambda a: moba_query(a[0], a[1], kb, vb, kmean, rel_bias), (qc, qpos))
        return oc.reshape(s, N_HEADS, HEAD_DIM)

    return lax.map(per_seq, (q, k, v))


def moba_sample(q, k_all, v_all, qpos, rel_bias):
    def per_seq(args):
        qs, ks, vs = args
        kb = moba_blocks(ks)
        vb = moba_blocks(vs)
        kmean = jnp.mean(kb.astype(jnp.float32), axis=2)
        return moba_query(qs, qpos, kb, vb, kmean, rel_bias)

    return lax.map(per_seq, (q, k_all, v_all))


def setup_inputs(seed: int = 0) -> dict:
    key = jax.random.key(seed)
    ks = jax.random.split(key, 32)
    n_pages = PAST_LEN // PAGE_SIZE
    n_used = DEC_BATCH * n_pages
    n_phys = (5 * n_used + 3) // 4

    def nrm(k, shape, scale=1.0):
        return jax.random.normal(k, shape, jnp.float32) * scale

    page_table = jax.random.permutation(ks[0], n_phys)[:n_used].reshape(DEC_BATCH, n_pages).astype(jnp.int32)
    return {
        'x_prompt': nrm(ks[1], (BATCH, SEQ, D_MODEL)),
        'x_sample': nrm(ks[2], (DEC_BATCH, DEC_SEQ, D_MODEL)),
        'c_prompt': nrm(ks[3], (BATCH, D_MODEL)),
        'c_sample': nrm(ks[4], (DEC_BATCH, D_MODEL)),
        'cache_k': nrm(ks[5], (DEPTH, n_phys, PAGE_SIZE, N_HEADS, HEAD_DIM)),
        'cache_v': nrm(ks[6], (DEPTH, n_phys, PAGE_SIZE, N_HEADS, HEAD_DIM)),
        'state_conv': nrm(ks[7], (DEPTH, DEC_BATCH, CONV_W - 1, CONV_CH), 0.5),
        'page_table': page_table,
        'rel_bias': nrm(ks[8], (N_HEADS, N_BUCKETS), 0.5),
        'w_ada': nrm(ks[9], (DEPTH, D_MODEL, N_MOD * D_MODEL), 0.5 * D_MODEL ** -0.5),
        'b_ada': nrm(ks[10], (DEPTH, N_MOD * D_MODEL), 0.02),
        'g_norm': 1.0 + nrm(ks[11], (DEPTH, 3, D_MODEL), 0.02),
        'w_ffn_gate': nrm(ks[12], (DEPTH, 2, D_MODEL, D_FF), D_MODEL ** -0.5),
        'w_ffn_up': nrm(ks[13], (DEPTH, 2, D_MODEL, D_FF), D_MODEL ** -0.5),
        'w_ffn_down': nrm(ks[14], (DEPTH, 2, D_FF, D_MODEL), D_FF ** -0.5),
        'w_in': nrm(ks[15], (DEPTH, D_MODEL, IN_COLS), D_MODEL ** -0.5),
        'conv_w': nrm(ks[16], (DEPTH, CONV_W, CONV_CH), CONV_W ** -0.5),
        'conv_b': nrm(ks[17], (DEPTH, CONV_CH), 0.02),
        'conv_ln_g': 1.0 + nrm(ks[18], (DEPTH, CONV_CH), 0.02),
        'conv_ln_b': nrm(ks[19], (DEPTH, CONV_CH), 0.02),
        'w_conv_out': nrm(ks[20], (DEPTH, CONV_CH, D_MODEL), CONV_CH ** -0.5),
        'w_attn_out': nrm(ks[21], (DEPTH, ATTN_W, D_MODEL), ATTN_W ** -0.5),
        'w_o': nrm(ks[22], (DEPTH, D_MODEL, D_MODEL), D_MODEL ** -0.5),
        'g_final': 1.0 + nrm(ks[23], (D_MODEL,), 0.02),
    }


def reference(x_prompt, x_sample, c_prompt, c_sample, cache_k, cache_v, state_conv, page_table,
              rel_bias, w_ada, b_ada, g_norm, w_ffn_gate, w_ffn_up, w_ffn_down, w_in, conv_w, conv_b,
              conv_ln_g, conv_ln_b, w_conv_out, w_attn_out, w_o, g_final):

    def layer(x, c, l, mixer):
        n, t, _ = x.shape
        mod = jax.nn.silu(c) @ w_ada[l] + b_ada[l]
        sh1, sc1, gt1, sh2, sc2, gt2, sh3, sc3, gt3 = jnp.split(mod[:, None, :], N_MOD, axis=-1)
        h = modulate(x, g_norm[l, 0], sh1, sc1)
        x = x + HALF * gt1 * swiglu(h, w_ffn_gate[l, 0], w_ffn_up[l, 0], w_ffn_down[l, 0])
        h = modulate(x, g_norm[l, 1], sh2, sc2)
        a, g, q, k, v, ga, gb = jnp.split(h @ w_in[l], IN_SPLITS, axis=-1)
        u = a * jax.nn.sigmoid(g)
        q = q.reshape(n, t, N_HEADS, HEAD_DIM)
        k = k.reshape(n, t, N_HEADS, HEAD_DIM)
        v = v.reshape(n, t, N_HEADS, HEAD_DIM)
        cv, att, st = mixer(u, q, k, v, l)
        conv_y = jax.nn.silu(layernorm(cv + conv_b[l], conv_ln_g[l], conv_ln_b[l])) @ w_conv_out[l]
        attn_y = att.reshape(n, t, ATTN_W) @ w_attn_out[l]
        mixed = (jax.nn.sigmoid(ga) * conv_y + jax.nn.sigmoid(gb) * attn_y) @ w_o[l]
        x = x + gt2 * mixed
        h = modulate(x, g_norm[l, 2], sh3, sc3)
        x = x + HALF * gt3 * swiglu(h, w_ffn_gate[l, 1], w_ffn_up[l, 1], w_ffn_down[l, 1])
        return x, st

    def prompt_mixer(u, q, k, v, l):
        u_ext = jnp.pad(u, ((0, 0), (CONV_W - 1, 0), (0, 0)))
        cv = causal_dwconv(u_ext, conv_w[l])
        att = moba_prompt(q, k, v, rel_bias)
        return cv, att, (k, v, u[:, -(CONV_W - 1):])

    x = x_prompt
    kp_l, vp_l, cp_l = [], [], []
    for l in range(DEPTH):
        x, (k_l, v_l, c_l) = layer(x, c_prompt, l, prompt_mixer)
        kp_l.append(k_l)
        vp_l.append(v_l)
        cp_l.append(c_l)
    y_prompt = rmsnorm(x, g_final)

    n_past = page_table.shape[1] * cache_k.shape[2]
    qpos_s = n_past + jnp.arange(x_sample.shape[1], dtype=jnp.int32)

    def sample_mixer(u, q, k, v, l):
        nbat = u.shape[0]
        u_ext = jnp.concatenate([state_conv[l], u], axis=1)
        cv = causal_dwconv(u_ext, conv_w[l])
        k_past = cache_k[l, page_table].reshape(nbat, n_past, N_HEADS, HEAD_DIM)
        v_past = cache_v[l, page_table].reshape(nbat, n_past, N_HEADS, HEAD_DIM)
        k_all = jnp.concatenate([k_past, k], axis=1)
        v_all = jnp.concatenate([v_past, v], axis=1)
        att = moba_sample(q, k_all, v_all, qpos_s, rel_bias)
        return cv, att, (k, v, u_ext[:, -(CONV_W - 1):])

    x = x_sample
    ks_l, vs_l, cs_l = [], [], []
    for l in range(DEPTH):
        x, (k_l, v_l, c_l) = layer(x, c_sample, l, sample_mixer)
        ks_l.append(k_l)
        vs_l.append(v_l)
        cs_l.append(c_l)
    y_sample = rmsnorm(x, g_final)

    return (y_prompt, y_sample, jnp.stack(kp_l), jnp.stack(vp_l), jnp.stack(cp_l),
            jnp.stack(ks_l), jnp.stack(vs_l), jnp.stack(cs_l))
```

```python
import functools
import math

import jax
import jax.numpy as jnp
from jax import lax
from jax.experimental import pallas as pl
from jax.experimental.pallas import tpu as pltpu

N_HEADS = 8
HEAD_DIM = 64
MOBA_BLOCK = 256
MOBA_TOPK = 3
N_BUCKETS = 32
MAX_DISTANCE = 128
N_MOD = 9
HALF = 0.5
EPS = 1e-6
NEG = -1e30

LANES = 128
HEADS_PER_GROUP = LANES // HEAD_DIM
MXU_DIM = 256
TOKEN_TILE = 512
VMEM_LIMIT = 56 * 1024 * 1024

F32 = jnp.float32
BF16 = jnp.bfloat16


def _params(n_grid_dims):
    return pltpu.CompilerParams(dimension_semantics=("arbitrary",) * n_grid_dims,
                                vmem_limit_bytes=VMEM_LIMIT)


def _dot(a, b):
    return jnp.dot(a, b, preferred_element_type=F32)


def _dot_nt(a, b, precision=None):
    return lax.dot_general(a, b, (((1,), (1,)), ((), ())), precision=precision,
                           preferred_element_type=F32)


def _silu(x):
    return x * jax.nn.sigmoid(x)


def _modulated_norm(x, g, shift, scale):
    y = x * lax.rsqrt(jnp.mean(x * x, axis=-1, keepdims=True) + EPS) * g
    return y * (1.0 + scale) + shift


def _ada_kernel(c_ref, w_ref, b_ref, o_ref):
    a = _silu(c_ref[...]).astype(BF16)
    o_ref[...] = _dot(a, w_ref[...].astype(BF16)) + b_ref[...]


def _ada(c_all, w_ada, b_ada):
    depth, d, nd = w_ada.shape
    nc = c_all.shape[0]
    tn = nd // 6
    assert tn % LANES == 0
    return pl.pallas_call(
        _ada_kernel,
        grid=(depth, nd // tn),
        in_specs=[pl.BlockSpec((nc, d), lambda l, j: (0, 0)),
                  pl.BlockSpec((None, d, tn), lambda l, j: (l, 0, j)),
                  pl.BlockSpec((None, 1, tn), lambda l, j: (l, 0, j))],
        out_specs=pl.BlockSpec((None, nc, tn), lambda l, j: (l, 0, j)),
        out_shape=jax.ShapeDtypeStruct((depth, nc, nd), F32),
        compiler_params=_params(2),
        name="ada_mod",
    )(c_all, w_ada, b_ada.reshape(depth, 1, nd))


def _bias_kernel(rel_ref, o_ref):
    t = pl.program_id(0)
    h = pl.program_id(1)
    row = lax.broadcasted_iota(jnp.int32, (MOBA_BLOCK, MOBA_BLOCK), 0)
    col = lax.broadcasted_iota(jnp.int32, (MOBA_BLOCK, MOBA_BLOCK), 1)
    d = row - col + jnp.where(t == 0, MOBA_BLOCK, 0)
    n = jnp.maximum(d, 0)
    max_exact = N_BUCKETS // 2
    nf = jnp.maximum(n, 1).astype(F32)
    large = max_exact + (jnp.log(nf / max_exact) / math.log(MAX_DISTANCE / max_exact)
                         * (N_BUCKETS - max_exact)).astype(jnp.int32)
    large = jnp.minimum(large, N_BUCKETS - 1)
    bucket = jnp.where(n < max_exact, n, large)
    bias = jnp.zeros((MOBA_BLOCK, MOBA_BLOCK), F32)
    for b in range(N_BUCKETS):
        bias = jnp.where(bucket == b, rel_ref[h, b], bias)
    o_ref[...] = jnp.where(d < 0, NEG, bias)


def _bias_tiles(rel_bias):
    return pl.pallas_call(
        _bias_kernel,
        grid=(2, N_HEADS),
        in_specs=[pl.BlockSpec(memory_space=pltpu.SMEM)],
        out_specs=pl.BlockSpec((None, None, MOBA_BLOCK, MOBA_BLOCK), lambda t, h: (t, h, 0, 0)),
        out_shape=jax.ShapeDtypeStruct((2, N_HEADS, MOBA_BLOCK, MOBA_BLOCK), F32),
        compiler_params=_params(2),
        name="t5_bias_tiles",
    )(rel_bias)


def _mod_spec(m, tm, tiles_per_seq):
    if m.ndim == 3:
        return pl.BlockSpec((None, 1, m.shape[-1]), lambda i: (i // tiles_per_seq, 0, 0))
    return pl.BlockSpec((tm, m.shape[-1]), lambda i: (i, 0))


def _const_spec(a):
    zeros = (0,) * a.ndim
    return pl.BlockSpec(a.shape, lambda i: zeros)


def _ffn_kernel(final_norm, x_ref, sh_ref, sc_ref, gt_ref, g_ref, wg_ref, wu_ref, wd_ref, *rest):
    if final_norm:
        gf_ref, o_ref, h_scr, acc_scr = rest
    else:
        o_ref, h_scr, acc_scr = rest
    x = x_ref[...]
    h_scr[...] = _modulated_norm(x, g_ref[...], sh_ref[...], sc_ref[...]).astype(BF16)
    acc_scr[...] = jnp.zeros_like(acc_scr)

    def chunk(c, carry):
        h = h_scr[...]
        g = _dot(h, wg_ref[c])
        u = _dot(h, wu_ref[c])
        acc_scr[...] += _dot((_silu(g) * u).astype(BF16), wd_ref[c])
        return carry

    lax.fori_loop(0, wg_ref.shape[0], chunk, 0)
    y = x + (HALF * gt_ref[...]) * acc_scr[...]
    if final_norm:
        y = y * lax.rsqrt(jnp.mean(y * y, axis=-1, keepdims=True) + EPS) * gf_ref[...]
    o_ref[...] = y


def _ffn(x, sh, sc, gt, g, wg, wu, wd, tiles_per_seq, g_final=None):
    nt, d = x.shape
    tm = min(TOKEN_TILE, nt)
    args = [x, sh, sc, gt, g, wg, wu, wd]
    in_specs = [pl.BlockSpec((tm, d), lambda i: (i, 0)),
                _mod_spec(sh, tm, tiles_per_seq), _mod_spec(sc, tm, tiles_per_seq),
                _mod_spec(gt, tm, tiles_per_seq),
                _const_spec(g), _const_spec(wg), _const_spec(wu), _const_spec(wd)]
    if g_final is not None:
        args.append(g_final)
        in_specs.append(_const_spec(g_final))
    return pl.pallas_call(
        functools.partial(_ffn_kernel, g_final is not None),
        grid=(nt // tm,),
        in_specs=in_specs,
        out_specs=pl.BlockSpec((tm, d), lambda i: (i, 0)),
        out_shape=jax.ShapeDtypeStruct((nt, d), F32),
        scratch_shapes=[pltpu.VMEM((tm, d), BF16), pltpu.VMEM((tm, d), F32)],
        compiler_params=_params(1),
        name="swiglu_ffn",
    )(*args)


def _mixer_in_kernel(with_kmean, cw, aw, x_ref, sh_ref, sc_ref, g_ref, w_ref, u_ref, q_ref, k_ref,
                     v_ref, sga_ref, sgb_ref, *rest):
    d = x_ref.shape[-1]
    h = _modulated_norm(x_ref[...], g_ref[...], sh_ref[...], sc_ref[...]).astype(BF16)
    a = _dot(h, w_ref[:, 0:cw])
    g = _dot(h, w_ref[:, cw:2 * cw])
    u_ref[...] = a * jax.nn.sigmoid(g)
    o = 2 * cw
    q_ref[...] = _dot(h, w_ref[:, o:o + aw])
    k = _dot(h, w_ref[:, o + aw:o + 2 * aw])
    k_ref[...] = k
    v_ref[...] = _dot(h, w_ref[:, o + 2 * aw:o + 3 * aw])
    o += 3 * aw
    sga_ref[...] = jax.nn.sigmoid(_dot(h, w_ref[:, o:o + d]))
    sgb_ref[...] = jax.nn.sigmoid(_dot(h, w_ref[:, o + d:o + 2 * d]))
    if with_kmean:
        km_ref, = rest
        tm = k.shape[0]
        km_ref[...] = jnp.sum(k.reshape(tm // MOBA_BLOCK, MOBA_BLOCK, aw), axis=1) * (1.0 / MOBA_BLOCK)


def _mixer_in(x, sh, sc, g, w_in, cw, aw, tiles_per_seq, with_kmean):
    nt, d = x.shape
    tm = min(TOKEN_TILE, nt)
    row = lambda i: (i, 0)
    out_shape = [jax.ShapeDtypeStruct((nt, cw), F32)] + [jax.ShapeDtypeStruct((nt, aw), F32)] * 3 \
        + [jax.ShapeDtypeStruct((nt, d), F32)] * 2
    out_specs = [pl.BlockSpec((tm, cw), row)] + [pl.BlockSpec((tm, aw), row)] * 3 \
        + [pl.BlockSpec((tm, d), row)] * 2
    if with_kmean:
        nb = tm // MOBA_BLOCK
        out_shape.append(jax.ShapeDtypeStruct((nt // tm, nb, aw), F32))
        out_specs.append(pl.BlockSpec((None, nb, aw), lambda i: (i, 0, 0)))
    return pl.pallas_call(
        functools.partial(_mixer_in_kernel, with_kmean, cw, aw),
        grid=(nt // tm,),
        in_specs=[pl.BlockSpec((tm, d), row), _mod_spec(sh, tm, tiles_per_seq),
                  _mod_spec(sc, tm, tiles_per_seq), _const_spec(g), _const_spec(w_in)],
        out_specs=out_specs,
        out_shape=out_shape,
        compiler_params=_params(1),
        name="mixer_in_proj",
    )(x, sh, sc, g, w_in)


def _ln_swish(cv, g, b):
    mu = jnp.mean(cv, axis=-1, keepdims=True)
    xc = cv - mu
    y = xc * lax.rsqrt(jnp.mean(xc * xc, axis=-1, keepdims=True) + EPS) * g + b
    return _silu(y)


HALO = 32
CONV_ROWS = 64


def _conv_prompt_kernel(conv_w, cur_ref, halo_ref, w_ref, b_ref, g_ref, bb_ref, o_ref, ext_scr):
    i = pl.program_id(1)
    tt = cur_ref.shape[0]
    halo = halo_ref[...]
    ext_scr[0:HALO, :] = jnp.where(i > 0, halo, jnp.zeros_like(halo))
    ext_scr[HALO:HALO + tt, :] = cur_ref[...]
    base = HALO - (conv_w - 1)
    for r in range(tt // CONV_ROWS):
        acc = jnp.zeros((CONV_ROWS, cur_ref.shape[1]), F32)
        for j in range(conv_w):
            s = r * CONV_ROWS + base + j
            acc = acc + w_ref[j:j + 1, :] * ext_scr[s:s + CONV_ROWS, :]
        y = _ln_swish(acc + b_ref[...], g_ref[...], bb_ref[...])
        o_ref[r * CONV_ROWS:(r + 1) * CONV_ROWS, :] = y.astype(o_ref.dtype)


def _conv_prompt(u, conv_w, conv_b, ln_g, ln_b):
    b, s, c = u.shape
    cw = conv_w.shape[0]
    assert cw - 1 <= HALO
    tt = min(TOKEN_TILE, s)
    hb = tt // HALO
    vec = lambda a: a.reshape(1, c)
    const = lambda bi, i: (0, 0)
    return pl.pallas_call(
        functools.partial(_conv_prompt_kernel, cw),
        grid=(b, s // tt),
        in_specs=[pl.BlockSpec((None, tt, c), lambda bi, i: (bi, i, 0)),
                  pl.BlockSpec((None, HALO, c), lambda bi, i: (bi, jnp.maximum(i * hb - 1, 0), 0)),
                  pl.BlockSpec((cw, c), const), pl.BlockSpec((1, c), const),
                  pl.BlockSpec((1, c), const), pl.BlockSpec((1, c), const)],
        out_specs=pl.BlockSpec((None, tt, c), lambda bi, i: (bi, i, 0)),
        out_shape=jax.ShapeDtypeStruct((b, s, c), BF16),
        scratch_shapes=[pltpu.VMEM((HALO + tt, c), F32)],
        compiler_params=_params(2),
        name="conv_prompt",
    )(u, u, conv_w, vec(conv_b), vec(ln_g), vec(ln_b))


def _conv_sample_kernel(st_ref, u_ref, ws_ref, wu_ref, b_ref, g_ref, bb_ref, o_ref):
    st = st_ref[...]
    u = u_ref[...]
    for t in range(u.shape[1]):
        cv = jnp.sum(st * ws_ref[t][None], axis=1) + jnp.sum(u * wu_ref[t][None], axis=1)
        o_ref[t] = _ln_swish(cv + b_ref[...], g_ref[...], bb_ref[...])


def _conv_sample(state, u, conv_w, conv_b, ln_g, ln_b):
    n, hist, c = state.shape
    t = u.shape[1]
    cw = conv_w.shape[0]
    assert cw == hist + 1
    w_ext = jnp.stack([jnp.pad(conv_w, ((tt, t - 1 - tt), (0, 0))) for tt in range(t)])
    vec = lambda a: a.reshape(1, c)
    full = lambda a: pl.BlockSpec(a.shape, lambda i: (0,) * a.ndim)
    args = [state, u, w_ext[:, :hist], w_ext[:, hist:], vec(conv_b), vec(ln_g), vec(ln_b)]
    return pl.pallas_call(
        _conv_sample_kernel,
        grid=(1,),
        in_specs=[full(a) for a in args],
        out_specs=pl.BlockSpec((t, n, c), lambda i: (0, 0, 0)),
        out_shape=jax.ShapeDtypeStruct((t, n, c), F32),
        compiler_params=_params(1),
        name="conv_sample",
    )(*args)


def _select_blocks(gate, n_valid, rounds):
    ncol = lax.broadcasted_iota(jnp.int32, gate.shape, 1)
    ncol_f = ncol.astype(F32)
    past = ncol < n_valid
    g = jnp.where(past, gate, NEG)
    sel = jnp.zeros(gate.shape, F32)
    for _ in range(rounds):
        mx = jnp.max(g, axis=-1, keepdims=True)
        idx = jnp.min(jnp.where(g == mx, ncol_f, float(LANES)), axis=-1, keepdims=True)
        pick = ncol_f == idx
        sel = jnp.where(pick, 1.0, sel)
        g = jnp.where(pick, -jnp.inf, g)
    return jnp.where(past, sel, 0.0), ncol


def _column(sel, ncol, n):
    return jnp.sum(jnp.where(ncol == n, sel, 0.0), axis=-1, keepdims=True) > 0.5


def _attn_prompt_kernel(rounds, far_ref, q_ref, k_ref, v_ref, km_ref, bias_ref, o_ref, kb_scr, vb_scr):
    hp = pl.program_id(1)
    own = pl.program_id(2)

    @pl.when(own == 0)
    def _():
        kb_scr[...] = k_ref[...].astype(BF16)
        vb_scr[...] = v_ref[...].astype(BF16)

    q = q_ref[...]
    lane = lax.broadcasted_iota(jnp.int32, q.shape, 1)
    nb = km_ref.shape[0]
    km = jnp.concatenate([km_ref[...], jnp.zeros((LANES - nb, LANES), F32)], axis=0)
    own_rows = pl.ds(pl.multiple_of(own * MOBA_BLOCK, MOBA_BLOCK), MOBA_BLOCK)
    prev = jnp.maximum(own - 1, 0)
    prev_rows = pl.ds(pl.multiple_of(prev * MOBA_BLOCK, MOBA_BLOCK), MOBA_BLOCK)

    outs = []
    for hh in range(HEADS_PER_GROUP):
        in_head = (lane >= hh * HEAD_DIM) & (lane < (hh + 1) * HEAD_DIM)
        qh = jnp.where(in_head, q, 0.0)
        gate = _dot_nt(qh, km, precision=lax.Precision.HIGHEST)
        sel, ncol = _select_blocks(gate, own, rounds)
        qs = (qh * (HEAD_DIM ** -0.5)).astype(BF16)
        far_bias = far_ref[hp * HEADS_PER_GROUP + hh]

        s = _dot_nt(qs, kb_scr[own_rows, :]) + bias_ref[1, hh]
        m = jnp.max(s, axis=-1, keepdims=True)
        p = jnp.exp(s - m)
        l = jnp.sum(p, axis=-1, keepdims=True)
        acc = _dot(p.astype(BF16), vb_scr[own_rows, :])

        def attend(carry, rows, n, bias):
            m, l, acc = carry
            s = jnp.where(_column(sel, ncol, n), _dot_nt(qs, kb_scr[rows, :]) + bias, NEG)
            m_new = jnp.maximum(m, jnp.max(s, axis=-1, keepdims=True))
            alpha = jnp.exp(m - m_new)
            p = jnp.exp(s - m_new)
            l = alpha * l + jnp.sum(p, axis=-1, keepdims=True)
            acc = alpha * acc + _dot(p.astype(BF16), vb_scr[rows, :])
            return m_new, l, acc

        carry = attend((m, l, acc), prev_rows, prev, bias_ref[0, hh])

        def far(n, carry):
            rows = pl.ds(pl.multiple_of(n * MOBA_BLOCK, MOBA_BLOCK), MOBA_BLOCK)
            return attend(carry, rows, n, far_bias)

        m, l, acc = lax.fori_loop(0, own - 1, far, carry)
        outs.append(acc / l)

    out = outs[0]
    for hh in range(1, HEADS_PER_GROUP):
        out = jnp.where(lane >= hh * HEAD_DIM, outs[hh], out)
    o_ref[...] = out.astype(o_ref.dtype)


def _attn_prompt(q, k, v, kmean, bias_tiles, far_bias):
    b, s, aw = q.shape
    nb = s // MOBA_BLOCK
    ng = aw // LANES
    rounds = min(MOBA_TOPK, nb)
    grid_spec = pltpu.PrefetchScalarGridSpec(
        num_scalar_prefetch=1,
        grid=(b, ng, nb),
        in_specs=[pl.BlockSpec((None, MOBA_BLOCK, LANES), lambda bi, g, i, far: (bi, i, g)),
                  pl.BlockSpec((None, s, LANES), lambda bi, g, i, far: (bi, 0, g)),
                  pl.BlockSpec((None, s, LANES), lambda bi, g, i, far: (bi, 0, g)),
                  pl.BlockSpec((None, nb, LANES), lambda bi, g, i, far: (bi, 0, g)),
                  pl.BlockSpec((2, HEADS_PER_GROUP, MOBA_BLOCK, MOBA_BLOCK),
                               lambda bi, g, i, far: (0, g, 0, 0))],
        out_specs=pl.BlockSpec((None, MOBA_BLOCK, LANES), lambda bi, g, i, far: (bi, i, g)),
        scratch_shapes=[pltpu.VMEM((s, LANES), BF16), pltpu.VMEM((s, LANES), BF16)],
    )
    return pl.pallas_call(
        functools.partial(_attn_prompt_kernel, rounds),
        grid_spec=grid_spec,
        out_shape=jax.ShapeDtypeStruct((b, s, aw), BF16),
        compiler_params=_params(3),
        name="moba_prompt",
    )(far_bias, q, k, v, kmean, bias_tiles)


def _attn_sample_kernel(n_pages, page, rounds, pt_ref, q_ref, kn_ref, vn_ref, bprev_ref, bown_ref,
                        far_ref, *rest):
    kp_refs = rest[:n_pages]
    vp_refs = rest[n_pages:2 * n_pages]
    o_ref, kb_scr, vb_scr, km_scr = rest[2 * n_pages:]
    t, aw = q_ref.shape
    rows = t * N_HEADS
    ppb = MOBA_BLOCK // page
    n_past = n_pages // ppb

    km_scr[...] = jnp.zeros_like(km_scr)
    for n in range(n_past):
        ksum = jnp.zeros((1, aw), F32)
        for j in range(ppb):
            pg = n * ppb + j
            kpage = kp_refs[pg][...]
            ksum = ksum + jnp.sum(kpage, axis=0, keepdims=True)
            kb_scr[pg * page:(pg + 1) * page, :] = kpage.astype(BF16)
            vb_scr[pg * page:(pg + 1) * page, :] = vp_refs[pg][...].astype(BF16)
        km_scr[n:n + 1, :] = ksum * (1.0 / MOBA_BLOCK)

    r_i = lax.broadcasted_iota(jnp.int32, (rows, aw), 0)
    c_i = lax.broadcasted_iota(jnp.int32, (rows, aw), 1)
    in_head = (c_i // HEAD_DIM) == (r_i % N_HEADS)
    q = q_ref[...]
    q_rep = jnp.zeros((rows, aw), F32)
    for qi in range(t):
        q_rep = jnp.where(r_i // N_HEADS == qi, q[qi:qi + 1, :], q_rep)
    qbd = jnp.where(in_head, q_rep, 0.0)
    gate = _dot_nt(qbd, km_scr[...], precision=lax.Precision.HIGHEST)
    sel, ncol = _select_blocks(gate, n_past, rounds)
    qs = (qbd * (HEAD_DIM ** -0.5)).astype(BF16)

    logits = _dot_nt(qs, kb_scr[...])
    s_blocks = []
    for n in range(n_past):
        bias = bprev_ref[...] if n == n_past - 1 else far_ref[...]
        s_n = logits[:, n * MOBA_BLOCK:(n + 1) * MOBA_BLOCK] + bias
        s_blocks.append(jnp.where(_column(sel, ncol, n), s_n, NEG))
    pad = jnp.zeros((LANES - t, aw), F32)
    kn = jnp.concatenate([kn_ref[...], pad], axis=0).astype(BF16)
    vn = jnp.concatenate([vn_ref[...], pad], axis=0).astype(BF16)
    s_own = _dot_nt(qs, kn) + bown_ref[...]

    m = jnp.max(s_own, axis=-1, keepdims=True)
    for s_n in s_blocks:
        m = jnp.maximum(m, jnp.max(s_n, axis=-1, keepdims=True))
    p_own = jnp.exp(s_own - m)
    l = jnp.sum(p_own, axis=-1, keepdims=True)
    out = _dot(p_own.astype(BF16), vn)
    p_blocks = []
    for s_n in s_blocks:
        p_n = jnp.exp(s_n - m)
        l = l + jnp.sum(p_n, axis=-1, keepdims=True)
        p_blocks.append(p_n.astype(BF16))
    out = out + _dot(jnp.concatenate(p_blocks, axis=-1), vb_scr[...])
    out = jnp.where(in_head, out / l, 0.0)
    o_ref[...] = jnp.sum(out.reshape(t, N_HEADS, aw), axis=1)


def _attn_sample(layer, q, k_new, v_new, cache_k, cache_v, page_table, bprev, bown, far):
    n, t, aw = q.shape
    n_pages = page_table.shape[1]
    page = cache_k.shape[2]
    n_past_tokens = n_pages * page
    assert MOBA_BLOCK % page == 0 and n_past_tokens % MOBA_BLOCK == 0 and t <= MOBA_BLOCK
    n_blocks = n_past_tokens // MOBA_BLOCK + 1
    rounds = min(MOBA_TOPK, n_blocks)
    rows = t * N_HEADS
    tok = pl.BlockSpec((None, t, aw), lambda i, pt: (i, 0, 0))
    const2 = lambda a: pl.BlockSpec(a.shape, lambda i, pt: (0, 0))
    page_specs = [pl.BlockSpec((None, None, page, aw), lambda i, pt, p=p: (layer, pt[i, p], 0, 0))
                  for p in range(n_pages)]
    grid_spec = pltpu.PrefetchScalarGridSpec(
        num_scalar_prefetch=1,
        grid=(n,),
        in_specs=[tok, tok, tok, const2(bprev), const2(bown), const2(far)] + page_specs + page_specs,
        out_specs=pl.BlockSpec((None, t, aw), lambda i, pt: (i, 0, 0)),
        scratch_shapes=[pltpu.VMEM((n_past_tokens, aw), BF16), pltpu.VMEM((n_past_tokens, aw), BF16),
                        pltpu.VMEM((LANES, aw), F32)],
    )
    return pl.pallas_call(
        functools.partial(_attn_sample_kernel, n_pages, page, rounds),
        grid_spec=grid_spec,
        out_shape=jax.ShapeDtypeStruct((n, t, aw), F32),
        compiler_params=_params(1),
        name="moba_sample",
    )(page_table, q, k_new, v_new, bprev, bown, far, *([cache_k] * n_pages), *([cache_v] * n_pages))


def _mixer_out_kernel(x_ref, gt_ref, c_ref, a_ref, sga_ref, sgb_ref, wc_ref, wa_ref, wo_ref, o_ref):
    conv_y = _dot(c_ref[...].astype(BF16), wc_ref[...])
    attn_y = _dot(a_ref[...].astype(BF16), wa_ref[...])
    mixed = (sga_ref[...] * conv_y + sgb_ref[...] * attn_y).astype(BF16)
    o_ref[...] = x_ref[...] + gt_ref[...] * _dot(mixed, wo_ref[...])


def _mixer_out(x, gt, cact, att, sga, sgb, wc, wa, wo, tiles_per_seq):
    nt, d = x.shape
    tm = min(TOKEN_TILE, nt)
    row = lambda i: (i, 0)
    return pl.pallas_call(
        _mixer_out_kernel,
        grid=(nt // tm,),
        in_specs=[pl.BlockSpec((tm, d), row), _mod_spec(gt, tm, tiles_per_seq),
                  pl.BlockSpec((tm, cact.shape[1]), row), pl.BlockSpec((tm, att.shape[1]), row),
                  pl.BlockSpec((tm, d), row), pl.BlockSpec((tm, d), row),
                  _const_spec(wc), _const_spec(wa), _const_spec(wo)],
        out_specs=pl.BlockSpec((tm, d), row),
        out_shape=jax.ShapeDtypeStruct((nt, d), F32),
        compiler_params=_params(1),
        name="mixer_out",
    )(x, gt, cact, att, sga, sgb, wc, wa, wo)


def _chunked_cols(w):
    d, f = w.shape
    return w.astype(BF16).reshape(d, f // MXU_DIM, MXU_DIM).transpose(1, 0, 2)


def kernel(x_prompt, x_sample, c_prompt, c_sample, cache_k, cache_v, state_conv, page_table, rel_bias, w_ada, b_ada, g_norm, w_ffn_gate, w_ffn_up, w_ffn_down, w_in, conv_w, conv_b, conv_ln_g, conv_ln_b, w_conv_out, w_attn_out, w_o, g_final):
    b, s, d = x_prompt.shape
    n, t, _ = x_sample.shape
    depth = w_ada.shape[0]
    cw = conv_w.shape[2]
    aw = N_HEADS * HEAD_DIM
    hist = conv_w.shape[1] - 1
    d_ff = w_ffn_gate.shape[-1]

    nc = -(-(b + n) // 8) * 8
    c_all = jnp.concatenate([c_prompt, c_sample, jnp.zeros((nc - b - n, d), F32)], axis=0)
    mod = _ada(c_all, w_ada, b_ada).reshape(depth, nc, N_MOD, d)
    tiles = _bias_tiles(rel_bias)
    far_bias = rel_bias[:, N_BUCKETS - 1]

    rows = t * N_HEADS
    bprev_s = tiles[0, :, :t, :].transpose(1, 0, 2).reshape(rows, MOBA_BLOCK)
    bown_s = jnp.pad(tiles[1, :, :t, :t].transpose(1, 0, 2).reshape(rows, t),
                     ((0, 0), (0, LANES - t)), constant_values=NEG)
    far_s = jnp.broadcast_to(jnp.tile(far_bias, t)[:, None], (rows, MOBA_BLOCK))

    cache_k4 = cache_k.reshape(cache_k.shape[:3] + (aw,))
    cache_v4 = cache_v.reshape(cache_v.shape[:3] + (aw,))

    wg = [[_chunked_cols(w_ffn_gate[l, i]) for i in range(2)] for l in range(depth)]
    wu = [[_chunked_cols(w_ffn_up[l, i]) for i in range(2)] for l in range(depth)]
    wd = [[w_ffn_down[l, i].astype(BF16).reshape(d_ff // MXU_DIM, MXU_DIM, d) for i in range(2)]
          for l in range(depth)]
    w_in_b = w_in.astype(BF16)
    wc_b = w_conv_out.astype(BF16)
    wa_b = w_attn_out.astype(BF16)
    wo_b = w_o.astype(BF16)

    def run_group(x3, mods, tiles_per_seq, mixer):
        nseq, tlen, _ = x3.shape
        x = x3.reshape(nseq * tlen, d)
        states = []
        for l in range(depth):
            mm = mods(l)
            gn = g_norm[l]
            last = l == depth - 1
            x = _ffn(x, mm[0], mm[1], mm[2], gn[0:1], wg[l][0], wu[l][0], wd[l][0], tiles_per_seq)
            u, q, k, v, sga, sgb, *km = _mixer_in(x, mm[3], mm[4], gn[1:2], w_in_b[l], cw, aw,
                                                  tiles_per_seq, mixer == "prompt")
            if mixer == "prompt":
                u3 = u.reshape(nseq, tlen, cw)
                cact = _conv_prompt(u3, conv_w[l], conv_b[l], conv_ln_g[l], conv_ln_b[l])
                att = _attn_prompt(q.reshape(nseq, tlen, aw), k.reshape(nseq, tlen, aw),
                                   v.reshape(nseq, tlen, aw),
                                   km[0].reshape(nseq, tlen // MOBA_BLOCK, aw), tiles, far_bias)
                st = u3[:, tlen - hist:]
                cact = cact.reshape(nseq * tlen, cw)
                att = att.reshape(nseq * tlen, aw)
            else:
                u3 = u.reshape(nseq, tlen, cw)
                cact = _conv_sample(state_conv[l], u3, conv_w[l], conv_b[l], conv_ln_g[l],
                                    conv_ln_b[l])
                cact = cact.transpose(1, 0, 2).reshape(nseq * tlen, cw)
                att = _attn_sample(l, q.reshape(nseq, tlen, aw), k.reshape(nseq, tlen, aw),
                                   v.reshape(nseq, tlen, aw), cache_k4, cache_v4, page_table,
                                   bprev_s, bown_s, far_s).reshape(nseq * tlen, aw)
                st = jnp.concatenate([state_conv[l], u3], axis=1)[:, tlen:]
            x = _mixer_out(x, mm[5], cact, att, sga, sgb, wc_b[l], wa_b[l], wo_b[l], tiles_per_seq)
            x = _ffn(x, mm[6], mm[7], mm[8], gn[2:3], wg[l][1], wu[l][1], wd[l][1], tiles_per_seq,
                     g_final.reshape(1, d) if last else None)
            states.append((k.reshape(nseq, tlen, N_HEADS, HEAD_DIM),
                           v.reshape(nseq, tlen, N_HEADS, HEAD_DIM), st))
        ks, vs, cs = (jnp.stack(z) for z in zip(*states))
        return x.reshape(nseq, tlen, d), ks, vs, cs

    tm_p = min(TOKEN_TILE, b * s)
    assert s % tm_p == 0 and tm_p % MOBA_BLOCK == 0
    y_p, k_p, v_p, c_p = run_group(
        x_prompt, lambda l: [mod[l, :b, i][:, None, :] for i in range(N_MOD)], s // tm_p, "prompt")
    y_s, k_s, v_s, c_s = run_group(
        x_sample, lambda l: [jnp.repeat(mod[l, b:b + n, i], t, axis=0) for i in range(N_MOD)],
        1, "sample")
    return y_p, y_s, k_p, v_p, c_p, k_s, v_s, c_s
```

```python
import functools
import math

import jax
import jax.numpy as jnp
from jax import lax
from jax.experimental import pallas as pl
from jax.experimental.pallas import tpu as pltpu

N_HEADS = 8
HEAD_DIM = 64
MOBA_BLOCK = 256
MOBA_TOPK = 3
N_BUCKETS = 32
MAX_DISTANCE = 128
N_MOD = 9
HALF = 0.5
EPS = 1e-6
NEG = -1e30

LANES = 128
HEADS_PER_GROUP = LANES // HEAD_DIM
MXU_DIM = 256
TOKEN_TILE = 512
VMEM_LIMIT = 56 * 1024 * 1024

F32 = jnp.float32
BF16 = jnp.bfloat16


def _params(n_grid_dims):
    return pltpu.CompilerParams(dimension_semantics=("arbitrary",) * n_grid_dims,
                                vmem_limit_bytes=VMEM_LIMIT)


def _dot(a, b):
    return jnp.dot(a, b, preferred_element_type=F32)


def _dot_nt(a, b, precision=None):
    return lax.dot_general(a, b, (((1,), (1,)), ((), ())), precision=precision,
                           preferred_element_type=F32)


def _silu(x):
    return x * jax.nn.sigmoid(x)


def _modulated_norm(x, g, shift, scale):
    y = x * lax.rsqrt(jnp.mean(x * x, axis=-1, keepdims=True) + EPS) * g
    return y * (1.0 + scale) + shift


def _ada_kernel(c_ref, w_ref, b_ref, o_ref):
    a = _silu(c_ref[...]).astype(BF16)
    o_ref[...] = _dot(a, w_ref[...].astype(BF16)) + b_ref[...]


def _ada(c_all, w_ada, b_ada):
    depth, d, nd = w_ada.shape
    nc = c_all.shape[0]
    tn = nd // 6
    assert tn % LANES == 0
    return pl.pallas_call(
        _ada_kernel,
        grid=(depth, nd // tn),
        in_specs=[pl.BlockSpec((nc, d), lambda l, j: (0, 0)),
                  pl.BlockSpec((None, d, tn), lambda l, j: (l, 0, j)),
                  pl.BlockSpec((None, 1, tn), lambda l, j: (l, 0, j))],
        out_specs=pl.BlockSpec((None, nc, tn), lambda l, j: (l, 0, j)),
        out_shape=jax.ShapeDtypeStruct((depth, nc, nd), F32),
        compiler_params=_params(2),
        name="ada_mod",
    )(c_all, w_ada, b_ada.reshape(depth, 1, nd))


def _bias_kernel(rel_ref, o_ref):
    t = pl.program_id(0)
    h = pl.program_id(1)
    row = lax.broadcasted_iota(jnp.int32, (MOBA_BLOCK, MOBA_BLOCK), 0)
    col = lax.broadcasted_iota(jnp.int32, (MOBA_BLOCK, MOBA_BLOCK), 1)
    d = row - col + jnp.where(t == 0, MOBA_BLOCK, 0)
    n = jnp.maximum(d, 0)
    max_exact = N_BUCKETS // 2
    nf = jnp.maximum(n, 1).astype(F32)
    large = max_exact + (jnp.log(nf / max_exact) / math.log(MAX_DISTANCE / max_exact)
                         * (N_BUCKETS - max_exact)).astype(jnp.int32)
    large = jnp.minimum(large, N_BUCKETS - 1)
    bucket = jnp.where(n < max_exact, n, large)
    bias = jnp.zeros((MOBA_BLOCK, MOBA_BLOCK), F32)
    for b in range(N_BUCKETS):
        bias = jnp.where(bucket == b, rel_ref[h, b], bias)
    o_ref[...] = jnp.where(d < 0, NEG, bias)


def _bias_tiles(rel_bias):
    return pl.pallas_call(
        _bias_kernel,
        grid=(2, N_HEADS),
        in_specs=[pl.BlockSpec(memory_space=pltpu.SMEM)],
        out_specs=pl.BlockSpec((None, None, MOBA_BLOCK, MOBA_BLOCK), lambda t, h: (t, h, 0, 0)),
        out_shape=jax.ShapeDtypeStruct((2, N_HEADS, MOBA_BLOCK, MOBA_BLOCK), F32),
        compiler_params=_params(2),
        name="t5_bias_tiles",
    )(rel_bias)


def _mod_spec(m, tm, tiles_per_seq):
    if m.ndim == 3:
        return pl.BlockSpec((None, 1, m.shape[-1]), lambda i: (i // tiles_per_seq, 0, 0))
    return pl.BlockSpec((tm, m.shape[-1]), lambda i: (i, 0))


def _const_spec(a):
    zeros = (0,) * a.ndim
    return pl.BlockSpec(a.shape, lambda i: zeros)


def _ffn_kernel(final_norm, x_ref, sh_ref, sc_ref, gt_ref, g_ref, wg_ref, wu_ref, wd_ref, *rest):
    if final_norm:
        gf_ref, o_ref, h_scr, acc_scr = rest
    else:
        o_ref, h_scr, acc_scr = rest
    x = x_ref[...]
    h_scr[...] = _modulated_norm(x, g_ref[...], sh_ref[...], sc_ref[...]).astype(BF16)
    acc_scr[...] = jnp.zeros_like(acc_scr)

    def chunk(c, carry):
        h = h_scr[...]
        g = _dot(h, wg_ref[c])
        u = _dot(h, wu_ref[c])
        acc_scr[...] += _dot((_silu(g) * u).astype(BF16), wd_ref[c])
        return carry

    lax.fori_loop(0, wg_ref.shape[0], chunk, 0)
    y = x + (HALF * gt_ref[...]) * acc_scr[...]
    if final_norm:
        y = y * lax.rsqrt(jnp.mean(y * y, axis=-1, keepdims=True) + EPS) * gf_ref[...]
    o_ref[...] = y


def _ffn(x, sh, sc, gt, g, wg, wu, wd, tiles_per_seq, g_final=None):
    nt, d = x.shape
    tm = min(TOKEN_TILE, nt)
    args = [x, sh, sc, gt, g, wg, wu, wd]
    in_specs = [pl.BlockSpec((tm, d), lambda i: (i, 0)),
                _mod_spec(sh, tm, tiles_per_seq), _mod_spec(sc, tm, tiles_per_seq),
                _mod_spec(gt, tm, tiles_per_seq),
                _const_spec(g), _const_spec(wg), _const_spec(wu), _const_spec(wd)]
    if g_final is not None:
        args.append(g_final)
        in_specs.append(_const_spec(g_final))
    return pl.pallas_call(
        functools.partial(_ffn_kernel, g_final is not None),
        grid=(nt // tm,),
        in_specs=in_specs,
        out_specs=pl.BlockSpec((tm, d), lambda i: (i, 0)),
        out_shape=jax.ShapeDtypeStruct((nt, d), F32),
        scratch_shapes=[pltpu.VMEM((tm, d), BF16), pltpu.VMEM((tm, d), F32)],
        compiler_params=_params(1),
        name="swiglu_ffn",
    )(*args)


def _mixer_in_kernel(transposed_kv, cw, aw, x_ref, sh_ref, sc_ref, g_ref, w_ref, *rest):
    if transposed_kv:
        wkv_ref, u_ref, q_ref, k_ref, v_ref, sga_ref, sgb_ref = rest
    else:
        u_ref, q_ref, k_ref, v_ref, sga_ref, sgb_ref = rest
    d = x_ref.shape[-1]
    h = _modulated_norm(x_ref[...], g_ref[...], sh_ref[...], sc_ref[...]).astype(BF16)
    a = _dot(h, w_ref[:, 0:cw])
    g = _dot(h, w_ref[:, cw:2 * cw])
    u_ref[...] = a * jax.nn.sigmoid(g)
    o = 2 * cw
    q_ref[...] = _dot(h, w_ref[:, o:o + aw])
    if transposed_kv:
        k_ref[...] = _dot_nt(wkv_ref[0:aw, :], h)
        v_ref[...] = _dot_nt(wkv_ref[aw:2 * aw, :], h)
    else:
        k_ref[...] = _dot(h, w_ref[:, o + aw:o + 2 * aw])
        v_ref[...] = _dot(h, w_ref[:, o + 2 * aw:o + 3 * aw])
    o += 3 * aw
    sga_ref[...] = jax.nn.sigmoid(_dot(h, w_ref[:, o:o + d]))
    sgb_ref[...] = jax.nn.sigmoid(_dot(h, w_ref[:, o + d:o + 2 * d]))


def _mixer_in(x, sh, sc, g, w_in, cw, aw, tiles_per_seq, w_kv_t=None):
    nt, d = x.shape
    tm = min(TOKEN_TILE, nt)
    row = lambda i: (i, 0)
    transposed_kv = w_kv_t is not None
    if transposed_kv:
        nseq = nt // (tiles_per_seq * tm)
        kv_shape = jax.ShapeDtypeStruct((nseq, aw, tiles_per_seq * tm), F32)
        kv_spec = pl.BlockSpec((None, aw, tm), lambda i: (i // tiles_per_seq, 0, i % tiles_per_seq))
    else:
        kv_shape = jax.ShapeDtypeStruct((nt, aw), F32)
        kv_spec = pl.BlockSpec((tm, aw), row)
    out_shape = [jax.ShapeDtypeStruct((nt, cw), F32), jax.ShapeDtypeStruct((nt, aw), F32),
                 kv_shape, kv_shape] + [jax.ShapeDtypeStruct((nt, d), F32)] * 2
    out_specs = [pl.BlockSpec((tm, cw), row), pl.BlockSpec((tm, aw), row), kv_spec, kv_spec] \
        + [pl.BlockSpec((tm, d), row)] * 2
    args = [x, sh, sc, g, w_in] + ([w_kv_t] if transposed_kv else [])
    in_specs = [pl.BlockSpec((tm, d), row), _mod_spec(sh, tm, tiles_per_seq),
                _mod_spec(sc, tm, tiles_per_seq), _const_spec(g), _const_spec(w_in)] \
        + ([_const_spec(w_kv_t)] if transposed_kv else [])
    return pl.pallas_call(
        functools.partial(_mixer_in_kernel, transposed_kv, cw, aw),
        grid=(nt // tm,),
        in_specs=in_specs,
        out_specs=out_specs,
        out_shape=out_shape,
        compiler_params=_params(1),
        name="mixer_in_proj",
    )(*args)


def _ln_swish(cv, g, b):
    mu = jnp.mean(cv, axis=-1, keepdims=True)
    xc = cv - mu
    y = xc * lax.rsqrt(jnp.mean(xc * xc, axis=-1, keepdims=True) + EPS) * g + b
    return _silu(y)


HALO = 32
CONV_ROWS = 64


def _conv_prompt_kernel(conv_w, cur_ref, halo_ref, w_ref, b_ref, g_ref, bb_ref, o_ref, ext_scr):
    i = pl.program_id(1)
    tt = cur_ref.shape[0]
    halo = halo_ref[...]
    ext_scr[0:HALO, :] = jnp.where(i > 0, halo, jnp.zeros_like(halo))
    ext_scr[HALO:HALO + tt, :] = cur_ref[...]
    base = HALO - (conv_w - 1)
    for r in range(tt // CONV_ROWS):
        acc = jnp.zeros((CONV_ROWS, cur_ref.shape[1]), F32)
        for j in range(conv_w):
            s = r * CONV_ROWS + base + j
            acc = acc + w_ref[j:j + 1, :] * ext_scr[s:s + CONV_ROWS, :]
        y = _ln_swish(acc + b_ref[...], g_ref[...], bb_ref[...])
        o_ref[r * CONV_ROWS:(r + 1) * CONV_ROWS, :] = y.astype(o_ref.dtype)


def _conv_prompt(u, conv_w, conv_b, ln_g, ln_b):
    b, s, c = u.shape
    cw = conv_w.shape[0]
    assert cw - 1 <= HALO
    tt = min(TOKEN_TILE, s)
    hb = tt // HALO
    vec = lambda a: a.reshape(1, c)
    const = lambda bi, i: (0, 0)
    return pl.pallas_call(
        functools.partial(_conv_prompt_kernel, cw),
        grid=(b, s // tt),
        in_specs=[pl.BlockSpec((None, tt, c), lambda bi, i: (bi, i, 0)),
                  pl.BlockSpec((None, HALO, c), lambda bi, i: (bi, jnp.maximum(i * hb - 1, 0), 0)),
                  pl.BlockSpec((cw, c), const), pl.BlockSpec((1, c), const),
                  pl.BlockSpec((1, c), const), pl.BlockSpec((1, c), const)],
        out_specs=pl.BlockSpec((None, tt, c), lambda bi, i: (bi, i, 0)),
        out_shape=jax.ShapeDtypeStruct((b, s, c), BF16),
        scratch_shapes=[pltpu.VMEM((HALO + tt, c), F32)],
        compiler_params=_params(2),
        name="conv_prompt",
    )(u, u, conv_w, vec(conv_b), vec(ln_g), vec(ln_b))


SEQ_CHUNK = 32


def _conv_sample_kernel(hist, st_ref, u_ref, w_ref, b_ref, g_ref, bb_ref, o_ref, ns_ref):
    t_new = u_ref.shape[0]
    row = lambda r: st_ref[r] if r < hist else u_ref[r - hist]
    for t in range(t_new):
        acc = jnp.zeros(u_ref.shape[1:], F32)
        for j in range(w_ref.shape[0]):
            acc = acc + w_ref[j:j + 1, :] * row(t + j)
        o_ref[t] = _ln_swish(acc + b_ref[...], g_ref[...], bb_ref[...])
    for r in range(hist):
        ns_ref[r] = row(r + t_new)


def _conv_sample(layer, state_t, u_t, conv_w, conv_b, ln_g, ln_b):
    _, hist, n, c = state_t.shape
    t = u_t.shape[0]
    assert conv_w.shape[0] == hist + 1
    sc = min(SEQ_CHUNK, n)
    vec = lambda a: a.reshape(1, c)
    const = lambda i: (0, 0)
    return pl.pallas_call(
        functools.partial(_conv_sample_kernel, hist),
        grid=(n // sc,),
        in_specs=[pl.BlockSpec((None, hist, sc, c), lambda i: (layer, 0, i, 0)),
                  pl.BlockSpec((t, sc, c), lambda i: (0, i, 0)),
                  pl.BlockSpec(conv_w.shape, const), pl.BlockSpec((1, c), const),
                  pl.BlockSpec((1, c), const), pl.BlockSpec((1, c), const)],
        out_specs=[pl.BlockSpec((t, sc, c), lambda i: (0, i, 0)),
                   pl.BlockSpec((hist, sc, c), lambda i: (0, i, 0))],
        out_shape=[jax.ShapeDtypeStruct((t, n, c), F32), jax.ShapeDtypeStruct((hist, n, c), F32)],
        compiler_params=_params(1),
        name="conv_sample",
    )(state_t, u_t, conv_w, vec(conv_b), vec(ln_g), vec(ln_b))


def _select_blocks(gate, n_valid, rounds):
    ncol = lax.broadcasted_iota(jnp.int32, gate.shape, 1)
    ncol_f = ncol.astype(F32)
    past = ncol < n_valid
    g = jnp.where(past, gate, NEG)
    sel = jnp.zeros(gate.shape, F32)
    for _ in range(rounds):
        mx = jnp.max(g, axis=-1, keepdims=True)
        idx = jnp.min(jnp.where(g == mx, ncol_f, float(LANES)), axis=-1, keepdims=True)
        pick = ncol_f == idx
        sel = jnp.where(pick, 1.0, sel)
        g = jnp.where(pick, -jnp.inf, g)
    return jnp.where(past, sel, 0.0), ncol


def _column(sel, ncol, n):
    return jnp.sum(jnp.where(ncol == n, sel, 0.0), axis=-1, keepdims=True) > 0.5


SLOTS = 16
PAD_SLOT = 3 * SLOTS
FAR_GROUP = 4
NEG_BIG = -(2.0 ** 100)


def _attn_prompt_kernel(rounds, nb, far_ref, q_ref, kt_ref, vt_ref, bias_ref, o_ref,
                        kx_scr, vx_scr, kmx_scr):
    hp = pl.program_id(1)
    own = pl.program_id(2)
    blk = MOBA_BLOCK

    @pl.when(own == 0)
    def _():
        row = lax.broadcasted_iota(jnp.int32, (LANES, blk), 0)
        jrel = row % HEAD_DIM
        for hh in range(HEADS_PER_GROUP):
            spare = (row // HEAD_DIM) != hh
            kx_scr[hh, 0] = jnp.where(spare & (jrel == PAD_SLOT), 1.0, 0.0).astype(BF16)
            for n in range(nb):
                ind = jnp.where((jrel < PAD_SLOT) & (jrel % SLOTS == n), 1.0, 0.0)
                kx_scr[hh, n + 1] = jnp.where(spare, ind, kt_ref[:, n * blk:(n + 1) * blk]).astype(BF16)
        vx_scr[0] = jnp.zeros((LANES, blk), BF16)
        r2 = lax.broadcasted_iota(jnp.int32, (LANES, LANES), 0)
        c2 = lax.broadcasted_iota(jnp.int32, (LANES, LANES), 1)
        kmcols = jnp.zeros((LANES, LANES), F32)
        for n in range(nb):
            vx_scr[n + 1] = vt_ref[:, n * blk:(n + 1) * blk].astype(BF16)
            col = jnp.sum(kt_ref[:, n * blk:(n + 1) * blk], axis=1, keepdims=True) * (1.0 / blk)
            kmcols = jnp.where(c2 == n, col, kmcols)
        pick = jnp.where((r2 % HEAD_DIM) == c2, 1.0, 0.0)
        pick = jnp.where((r2 % HEAD_DIM) < SLOTS, pick, 0.0)
        kmx = _dot_nt(pick, kmcols, precision=lax.Precision.HIGHEST)
        kmx_scr[...] = jnp.where((r2 // HEAD_DIM) != (c2 // HEAD_DIM), kmx, 0.0)

    q = q_ref[...]
    gate_t = _dot_nt(kmx_scr[...], q, precision=lax.Precision.HIGHEST)
    n_i = lax.broadcasted_iota(jnp.int32, (SLOTS, blk), 0)
    zeros = jnp.zeros((SLOTS, blk), F32)
    vals_far, vals_near = [], []
    for hh in range(HEADS_PER_GROUP):
        base = (1 - hh) * HEAD_DIM
        g = jnp.where(n_i < own, gate_t[base:base + SLOTS, :], NEG)
        rank = zeros
        for m in range(nb):
            gm = g[m:m + 1, :]
            tie = jnp.where(n_i > m, 1.0, 0.0)
            rank = rank + jnp.where(gm > g, 1.0, jnp.where(gm == g, tie, 0.0))
        sel = jnp.where(n_i < own, jnp.where(rank < rounds, 1.0, 0.0), 0.0)
        far_ok = jnp.where(n_i < own - 1, sel, 0.0) > 0.5
        head = hp * HEADS_PER_GROUP + hh
        far_hi = jnp.where(far_ok, far_ref[0, head], NEG_BIG)
        far_lo = jnp.where(far_ok, far_ref[1, head], 0.0)
        near = jnp.where(n_i == own - 1, jnp.where(sel > 0.5, 0.0, NEG_BIG), 0.0)
        pad = jnp.where(n_i == 0, NEG_BIG, 0.0)
        vals_far.append(jnp.concatenate([far_hi, far_lo, zeros, zeros], axis=0))
        vals_near.append(jnp.concatenate([zeros, zeros, near, pad], axis=0))
    r_q = lax.broadcasted_iota(jnp.int32, (blk, blk), 0)
    c_q = lax.broadcasted_iota(jnp.int32, (blk, blk), 1)
    eye = jnp.where(r_q == c_q, 1.0, 0.0).astype(BF16)
    sp_far = _dot_nt(eye, jnp.concatenate(vals_far[::-1], axis=0).astype(BF16))
    sp_near = _dot_nt(eye, jnp.concatenate(vals_near[::-1], axis=0).astype(BF16))
    lane = lax.broadcasted_iota(jnp.int32, q.shape, 1)
    qs = q * (HEAD_DIM ** -0.5)

    q_far, carry = [], []
    for hh in range(HEADS_PER_GROUP):
        in_head = (lane // HEAD_DIM) == hh
        q_far.append(jnp.where(in_head, qs, sp_far).astype(BF16))
        q_near = jnp.where(in_head, qs, sp_near).astype(BF16)
        s_prev = _dot(q_near, kx_scr[hh, own]) + bias_ref[0, hh]
        s_own = _dot(q_near, kx_scr[hh, own + 1]) + bias_ref[1, hh]
        m = jnp.max(jnp.maximum(s_prev, s_own), axis=-1, keepdims=True)
        p_prev = jnp.exp(s_prev - m)
        p_own = jnp.exp(s_own - m)
        l = jnp.sum(p_prev + p_own, axis=-1, keepdims=True)
        acc = _dot_nt(p_prev.astype(BF16), vx_scr[own]) + _dot_nt(p_own.astype(BF16), vx_scr[own + 1])
        carry.append((m, l, acc))

    def far(gi, carry):
        out = []
        first = 1 + gi * FAR_GROUP
        for hh in range(HEADS_PER_GROUP):
            m, l, acc = carry[hh]
            ss = [_dot(q_far[hh], kx_scr[hh, first + j]) for j in range(FAR_GROUP)]
            mx = ss[0]
            for s in ss[1:]:
                mx = jnp.maximum(mx, s)
            m_new = jnp.maximum(m, jnp.max(mx, axis=-1, keepdims=True))
            alpha = jnp.exp(m - m_new)
            ps = [jnp.exp(s - m_new) for s in ss]
            psum = ps[0]
            for p in ps[1:]:
                psum = psum + p
            l = alpha * l + jnp.sum(psum, axis=-1, keepdims=True)
            pv = _dot_nt(ps[0].astype(BF16), vx_scr[first])
            for j in range(1, FAR_GROUP):
                pv = pv + _dot_nt(ps[j].astype(BF16), vx_scr[first + j])
            out.append((m_new, l, alpha * acc + pv))
        return tuple(out)

    carry = lax.fori_loop(0, (own + FAR_GROUP - 2) // FAR_GROUP, far, tuple(carry))
    out = carry[0][2] / carry[0][1]
    for hh in range(1, HEADS_PER_GROUP):
        out = jnp.where(lane >= hh * HEAD_DIM, carry[hh][2] / carry[hh][1], out)
    o_ref[...] = out.astype(o_ref.dtype)


def _attn_prompt(q, k_t, v_t, bias_tiles, far_parts):
    b, s, aw = q.shape
    nb = s // MOBA_BLOCK
    ng = aw // LANES
    assert nb <= SLOTS and nb % FAR_GROUP == 0 and HEADS_PER_GROUP == 2
    rounds = min(MOBA_TOPK, nb)
    grid_spec = pltpu.PrefetchScalarGridSpec(
        num_scalar_prefetch=1,
        grid=(b, ng, nb),
        in_specs=[pl.BlockSpec((None, MOBA_BLOCK, LANES), lambda bi, g, i, far: (bi, i, g)),
                  pl.BlockSpec((None, LANES, s), lambda bi, g, i, far: (bi, g, 0)),
                  pl.BlockSpec((None, LANES, s), lambda bi, g, i, far: (bi, g, 0)),
                  pl.BlockSpec((2, HEADS_PER_GROUP, MOBA_BLOCK, MOBA_BLOCK),
                               lambda bi, g, i, far: (0, g, 0, 0))],
        out_specs=pl.BlockSpec((None, MOBA_BLOCK, LANES), lambda bi, g, i, far: (bi, i, g)),
        scratch_shapes=[pltpu.VMEM((HEADS_PER_GROUP, nb + 1, LANES, MOBA_BLOCK), BF16),
                        pltpu.VMEM((nb + 1, LANES, MOBA_BLOCK), BF16),
                        pltpu.VMEM((LANES, LANES), F32)],
    )
    return pl.pallas_call(
        functools.partial(_attn_prompt_kernel, rounds, nb),
        grid_spec=grid_spec,
        out_shape=jax.ShapeDtypeStruct((b, s, aw), BF16),
        compiler_params=_params(3),
        name="moba_prompt",
    )(far_parts, q, k_t, v_t, bias_tiles)


def _attn_sample_kernel(n_pages, page, rounds, pt_ref, q_ref, kn_ref, vn_ref, bprev_ref, bown_ref,
                        far_ref, *rest):
    kp_refs = rest[:n_pages]
    vp_refs = rest[n_pages:2 * n_pages]
    o_ref, = rest[2 * n_pages:]
    t, aw = q_ref.shape
    rows = t * N_HEADS
    ppb = MOBA_BLOCK // page
    n_past = n_pages // ppb

    lane = lax.broadcasted_iota(jnp.int32, (aw, LANES), 1)
    kmcols = jnp.zeros((aw, LANES), F32)
    for n in range(n_past):
        ksum = kp_refs[n * ppb][...]
        for j in range(1, ppb):
            ksum = ksum + kp_refs[n * ppb + j][...]
        col = jnp.sum(ksum, axis=1, keepdims=True) * (1.0 / MOBA_BLOCK)
        kmcols = jnp.where(lane == n, col, kmcols)

    r_i = lax.broadcasted_iota(jnp.int32, (rows, aw), 0)
    c_i = lax.broadcasted_iota(jnp.int32, (rows, aw), 1)
    in_head = (c_i // HEAD_DIM) == (r_i % N_HEADS)
    q = q_ref[...]
    q_rep = jnp.zeros((rows, aw), F32)
    for qi in range(t):
        q_rep = jnp.where(r_i // N_HEADS == qi, q[qi:qi + 1, :], q_rep)
    qbd = jnp.where(in_head, q_rep, 0.0)
    gate = jnp.dot(qbd, kmcols, precision=lax.Precision.HIGHEST, preferred_element_type=F32)
    sel, ncol = _select_blocks(gate, n_past, rounds)
    qs = (qbd * (HEAD_DIM ** -0.5)).astype(BF16)

    s_blocks = []
    for n in range(n_past):
        s_n = jnp.concatenate([_dot(qs, kp_refs[n * ppb + j][...].astype(BF16)) for j in range(ppb)],
                              axis=-1)
        s_n = s_n + (bprev_ref[...] if n == n_past - 1 else far_ref[...])
        s_blocks.append(jnp.where(_column(sel, ncol, n), s_n, NEG))
    pad = jnp.zeros((LANES - t, aw), F32)
    kn = jnp.concatenate([kn_ref[...], pad], axis=0).astype(BF16)
    vn = jnp.concatenate([vn_ref[...], pad], axis=0).astype(BF16)
    s_own = _dot_nt(qs, kn) + bown_ref[...]

    m = jnp.max(s_own, axis=-1, keepdims=True)
    for s_n in s_blocks:
        m = jnp.maximum(m, jnp.max(s_n, axis=-1, keepdims=True))
    p_own = jnp.exp(s_own - m)
    l = jnp.sum(p_own, axis=-1, keepdims=True)
    out = _dot(p_own.astype(BF16), vn)
    for n, s_n in enumerate(s_blocks):
        p_n = jnp.exp(s_n - m)
        l = l + jnp.sum(p_n, axis=-1, keepdims=True)
        p_n = p_n.astype(BF16)
        for j in range(ppb):
            out = out + _dot_nt(p_n[:, j * page:(j + 1) * page], vp_refs[n * ppb + j][...].astype(BF16))
    out = jnp.where(in_head, out / l, 0.0)
    o_ref[...] = jnp.sum(out.reshape(t, N_HEADS, aw), axis=1)


def _attn_sample(layer, q, k_new, v_new, cache_kt, cache_vt, page_table, bprev, bown, far):
    n, t, aw = q.shape
    n_pages = page_table.shape[1]
    page = cache_kt.shape[3]
    n_past_tokens = n_pages * page
    assert page == LANES and MOBA_BLOCK % page == 0 and n_past_tokens % MOBA_BLOCK == 0
    assert t <= LANES and n_past_tokens // MOBA_BLOCK <= LANES
    n_blocks = n_past_tokens // MOBA_BLOCK + 1
    rounds = min(MOBA_TOPK, n_blocks)
    tok = pl.BlockSpec((None, t, aw), lambda i, pt: (i, 0, 0))
    const2 = lambda a: pl.BlockSpec(a.shape, lambda i, pt: (0, 0))
    page_specs = [pl.BlockSpec((None, None, aw, page), lambda i, pt, p=p: (layer, pt[i, p], 0, 0))
                  for p in range(n_pages)]
    grid_spec = pltpu.PrefetchScalarGridSpec(
        num_scalar_prefetch=1,
        grid=(n,),
        in_specs=[tok, tok, tok, const2(bprev), const2(bown), const2(far)] + page_specs + page_specs,
        out_specs=pl.BlockSpec((None, t, aw), lambda i, pt: (i, 0, 0)),
    )
    return pl.pallas_call(
        functools.partial(_attn_sample_kernel, n_pages, page, rounds),
        grid_spec=grid_spec,
        out_shape=jax.ShapeDtypeStruct((n, t, aw), F32),
        compiler_params=_params(1),
        name="moba_sample",
    )(page_table, q, k_new, v_new, bprev, bown, far, *([cache_kt] * n_pages), *([cache_vt] * n_pages))


def _mixer_out_kernel(x_ref, gt_ref, c_ref, a_ref, sga_ref, sgb_ref, wc_ref, wa_ref, wo_ref, o_ref):
    conv_y = _dot(c_ref[...].astype(BF16), wc_ref[...])
    attn_y = _dot(a_ref[...].astype(BF16), wa_ref[...])
    mixed = (sga_ref[...] * conv_y + sgb_ref[...] * attn_y).astype(BF16)
    o_ref[...] = x_ref[...] + gt_ref[...] * _dot(mixed, wo_ref[...])


def _mixer_out(x, gt, cact, att, sga, sgb, wc, wa, wo, tiles_per_seq):
    nt, d = x.shape
    tm = min(TOKEN_TILE, nt)
    row = lambda i: (i, 0)
    return pl.pallas_call(
        _mixer_out_kernel,
        grid=(nt // tm,),
        in_specs=[pl.BlockSpec((tm, d), row), _mod_spec(gt, tm, tiles_per_seq),
                  pl.BlockSpec((tm, cact.shape[1]), row), pl.BlockSpec((tm, att.shape[1]), row),
                  pl.BlockSpec((tm, d), row), pl.BlockSpec((tm, d), row),
                  _const_spec(wc), _const_spec(wa), _const_spec(wo)],
        out_specs=pl.BlockSpec((tm, d), row),
        out_shape=jax.ShapeDtypeStruct((nt, d), F32),
        compiler_params=_params(1),
        name="mixer_out",
    )(x, gt, cact, att, sga, sgb, wc, wa, wo)


def _chunked_cols(w):
    d, f = w.shape
    return w.astype(BF16).reshape(d, f // MXU_DIM, MXU_DIM).transpose(1, 0, 2)


def kernel(x_prompt, x_sample, c_prompt, c_sample, cache_k, cache_v, state_conv, page_table, rel_bias, w_ada, b_ada, g_norm, w_ffn_gate, w_ffn_up, w_ffn_down, w_in, conv_w, conv_b, conv_ln_g, conv_ln_b, w_conv_out, w_attn_out, w_o, g_final):
    b, s, d = x_prompt.shape
    n, t, _ = x_sample.shape
    depth = w_ada.shape[0]
    cw = conv_w.shape[2]
    aw = N_HEADS * HEAD_DIM
    d_ff = w_ffn_gate.shape[-1]

    nc = -(-(b + n) // 8) * 8
    c_all = jnp.concatenate([c_prompt, c_sample, jnp.zeros((nc - b - n, d), F32)], axis=0)
    mod = _ada(c_all, w_ada, b_ada).reshape(depth, nc, N_MOD, d)
    tiles = _bias_tiles(rel_bias)
    far_bias = rel_bias[:, N_BUCKETS - 1]
    far_hi = far_bias.astype(BF16).astype(F32)
    far_parts = jnp.stack([far_hi, far_bias - far_hi])

    rows = t * N_HEADS
    bprev_s = tiles[0, :, :t, :].transpose(1, 0, 2).reshape(rows, MOBA_BLOCK)
    bown_s = jnp.pad(tiles[1, :, :t, :t].transpose(1, 0, 2).reshape(rows, t),
                     ((0, 0), (0, LANES - t)), constant_values=NEG)
    far_s = jnp.broadcast_to(jnp.tile(far_bias, t)[:, None], (rows, MOBA_BLOCK))

    cache_kt = cache_k.transpose(0, 1, 3, 4, 2).reshape(cache_k.shape[:2] + (aw, cache_k.shape[2]))
    cache_vt = cache_v.transpose(0, 1, 3, 4, 2).reshape(cache_v.shape[:2] + (aw, cache_v.shape[2]))
    state_t = state_conv.transpose(0, 2, 1, 3)
    hist = state_t.shape[1]

    wg = [[_chunked_cols(w_ffn_gate[l, i]) for i in range(2)] for l in range(depth)]
    wu = [[_chunked_cols(w_ffn_up[l, i]) for i in range(2)] for l in range(depth)]
    wd = [[w_ffn_down[l, i].astype(BF16).reshape(d_ff // MXU_DIM, MXU_DIM, d) for i in range(2)]
          for l in range(depth)]
    w_in_b = w_in.astype(BF16)
    w_kv_t = w_in[:, :, 2 * cw + aw:2 * cw + 3 * aw].transpose(0, 2, 1).astype(BF16)
    wc_b = w_conv_out.astype(BF16)
    wa_b = w_attn_out.astype(BF16)
    wo_b = w_o.astype(BF16)
    gf = g_final.reshape(1, d)

    def ffn(x, mm, l, i, tiles_per_seq, final):
        o = 6 * i
        return _ffn(x, mm[o], mm[o + 1], mm[o + 2], g_norm[l, 2 * i:2 * i + 1], wg[l][i], wu[l][i],
                    wd[l][i], tiles_per_seq, gf if final else None)

    tm_p = min(TOKEN_TILE, s)
    assert s % tm_p == 0 and s % MOBA_BLOCK == 0
    tps = s // tm_p
    x = x_prompt.reshape(b * s, d)
    kp, vp, cp = [], [], []
    for l in range(depth):
        mm = [mod[l, :b, i][:, None, :] for i in range(N_MOD)]
        x = ffn(x, mm, l, 0, tps, False)
        u, q, k_t, v_t, sga, sgb = _mixer_in(x, mm[3], mm[4], g_norm[l, 1:2], w_in_b[l], cw, aw, tps,
                                             w_kv_t[l])
        u3 = u.reshape(b, s, cw)
        cact = _conv_prompt(u3, conv_w[l], conv_b[l], conv_ln_g[l], conv_ln_b[l])
        att = _attn_prompt(q.reshape(b, s, aw), k_t, v_t, tiles, far_parts)
        x = _mixer_out(x, mm[5], cact.reshape(b * s, cw), att.reshape(b * s, aw), sga, sgb,
                       wc_b[l], wa_b[l], wo_b[l], tps)
        x = ffn(x, mm, l, 1, tps, l == depth - 1)
        kp.append(k_t)
        vp.append(v_t)
        cp.append(u3[:, s - hist:])
    y_p = x.reshape(b, s, d)
    to_cache = lambda z: jnp.stack(z).reshape(depth, b, N_HEADS, HEAD_DIM, s).transpose(0, 1, 4, 2, 3)
    k_p, v_p, c_p = to_cache(kp), to_cache(vp), jnp.stack(cp)

    x = x_sample.reshape(n * t, d)
    ks, vs, cs = [], [], []
    for l in range(depth):
        mm = [jnp.repeat(mod[l, b:b + n, i], t, axis=0) for i in range(N_MOD)]
        x = ffn(x, mm, l, 0, 1, False)
        u, q, k, v, sga, sgb = _mixer_in(x, mm[3], mm[4], g_norm[l, 1:2], w_in_b[l], cw, aw, 1)
        u_t = u.reshape(n, t, cw).transpose(1, 0, 2)
        cact_t, new_state = _conv_sample(l, state_t, u_t, conv_w[l], conv_b[l], conv_ln_g[l],
                                         conv_ln_b[l])
        att = _attn_sample(l, q.reshape(n, t, aw), k.reshape(n, t, aw), v.reshape(n, t, aw),
                           cache_kt, cache_vt, page_table, bprev_s, bown_s, far_s)
        x = _mixer_out(x, mm[5], cact_t.transpose(1, 0, 2).reshape(n * t, cw), att.reshape(n * t, aw),
                       sga, sgb, wc_b[l], wa_b[l], wo_b[l], 1)
        x = ffn(x, mm, l, 1, 1, l == depth - 1)
        ks.append(k.reshape(n, t, N_HEADS, HEAD_DIM))
        vs.append(v.reshape(n, t, N_HEADS, HEAD_DIM))
        cs.append(new_state)
    y_s = x.reshape(n, t, d)
    return (y_p, y_s, k_p, v_p, c_p, jnp.stack(ks), jnp.stack(vs),
            jnp.stack(cs).transpose(0, 2, 1, 3))
```

```python
import functools
import math

import jax
import jax.numpy as jnp
from jax import lax
from jax.experimental import pallas as pl
from jax.experimental.pallas import tpu as pltpu

N_HEADS = 8
HEAD_DIM = 64
MOBA_BLOCK = 256
MOBA_TOPK = 3
N_BUCKETS = 32
MAX_DISTANCE = 128
N_MOD = 9
HALF = 0.5
EPS = 1e-6
NEG = -1e30

LANES = 128
SUBLANES = 8
HEADS_PER_GROUP = LANES // HEAD_DIM
MXU_DIM = 256
TOKEN_TILE = 512
VMEM_LIMIT = 56 * 1024 * 1024

F32 = jnp.float32
BF16 = jnp.bfloat16


def _params(n_grid_dims):
    return pltpu.CompilerParams(dimension_semantics=("arbitrary",) * n_grid_dims,
                                vmem_limit_bytes=VMEM_LIMIT)


def _dot(a, b):
    return jnp.dot(a, b, preferred_element_type=F32)


def _dot_nt(a, b, precision=None):
    return lax.dot_general(a, b, (((1,), (1,)), ((), ())), precision=precision,
                           preferred_element_type=F32)


def _silu(x):
    return x * jax.nn.sigmoid(x)


def _modulated_norm(x, g, shift, scale):
    y = x * lax.rsqrt(jnp.mean(x * x, axis=-1, keepdims=True) + EPS) * g
    return y * (1.0 + scale) + shift


def _ada_kernel(c_ref, w_ref, b_ref, o_ref):
    a = _silu(c_ref[...]).astype(BF16)
    o_ref[...] = _dot(a, w_ref[...].astype(BF16)) + b_ref[...]


def _ada(c_all, w_ada, b_ada):
    depth, d, nd = w_ada.shape
    nc = c_all.shape[0]
    tn = nd // 6
    assert tn % LANES == 0
    return pl.pallas_call(
        _ada_kernel,
        grid=(depth, nd // tn),
        in_specs=[pl.BlockSpec((nc, d), lambda l, j: (0, 0)),
                  pl.BlockSpec((None, d, tn), lambda l, j: (l, 0, j)),
                  pl.BlockSpec((None, 1, tn), lambda l, j: (l, 0, j))],
        out_specs=pl.BlockSpec((None, nc, tn), lambda l, j: (l, 0, j)),
        out_shape=jax.ShapeDtypeStruct((depth, nc, nd), F32),
        compiler_params=_params(2),
        name="ada_mod",
    )(c_all, w_ada, b_ada.reshape(depth, 1, nd))


Q_BLOCKS = 2
NEAR_BLOCKS = 4


def _bias_kernel(rel_ref, o_ref):
    h = pl.program_id(0)
    shape = (Q_BLOCKS * MOBA_BLOCK, NEAR_BLOCKS * MOBA_BLOCK)
    row = lax.broadcasted_iota(jnp.int32, shape, 0)
    col = lax.broadcasted_iota(jnp.int32, shape, 1)
    d = row - col + (NEAR_BLOCKS - Q_BLOCKS) * MOBA_BLOCK
    n = jnp.maximum(d, 0)
    max_exact = N_BUCKETS // 2
    nf = jnp.maximum(n, 1).astype(F32)
    large = max_exact + (jnp.log(nf / max_exact) / math.log(MAX_DISTANCE / max_exact)
                         * (N_BUCKETS - max_exact)).astype(jnp.int32)
    large = jnp.minimum(large, N_BUCKETS - 1)
    bucket = jnp.where(n < max_exact, n, large)
    bias = jnp.zeros(shape, F32)
    for b in range(N_BUCKETS):
        bias = jnp.where(bucket == b, rel_ref[h, b], bias)
    o_ref[...] = jnp.where(d < 0, NEG, bias)


def _bias_tiles(rel_bias):
    shape = (Q_BLOCKS * MOBA_BLOCK, NEAR_BLOCKS * MOBA_BLOCK)
    return pl.pallas_call(
        _bias_kernel,
        grid=(N_HEADS,),
        in_specs=[pl.BlockSpec(memory_space=pltpu.SMEM)],
        out_specs=pl.BlockSpec((None,) + shape, lambda h: (h, 0, 0)),
        out_shape=jax.ShapeDtypeStruct((N_HEADS,) + shape, F32),
        compiler_params=_params(1),
        name="t5_bias_tiles",
    )(rel_bias)


def _mod_spec(m, tm, tiles_per_seq):
    if m.ndim == 3:
        return pl.BlockSpec((None, 1, m.shape[-1]), lambda i: (i // tiles_per_seq, 0, 0))
    return pl.BlockSpec((tm, m.shape[-1]), lambda i: (i, 0))


def _const_spec(a):
    zeros = (0,) * a.ndim
    return pl.BlockSpec(a.shape, lambda i: zeros)


def _ffn_kernel(final_norm, x_ref, sh_ref, sc_ref, gt_ref, g_ref, wg_ref, wu_ref, wd_ref, *rest):
    if final_norm:
        gf_ref, o_ref, h_scr, acc_scr = rest
    else:
        o_ref, h_scr, acc_scr = rest
    x = x_ref[...]
    h_scr[...] = _modulated_norm(x, g_ref[...], sh_ref[...], sc_ref[...]).astype(BF16)
    acc_scr[...] = jnp.zeros_like(acc_scr)

    def chunk(c, carry):
        cols = pl.ds(pl.multiple_of(c * MXU_DIM, MXU_DIM), MXU_DIM)
        h = h_scr[...]
        g = _dot(h, wg_ref[:, cols])
        u = _dot(h, wu_ref[:, cols])
        acc_scr[...] += _dot((_silu(g) * u).astype(BF16), wd_ref[cols, :])
        return carry

    lax.fori_loop(0, wg_ref.shape[1] // MXU_DIM, chunk, 0)
    y = x + (HALF * gt_ref[...]) * acc_scr[...]
    if final_norm:
        y = y * lax.rsqrt(jnp.mean(y * y, axis=-1, keepdims=True) + EPS) * gf_ref[...]
    o_ref[...] = y


def _ffn(x, sh, sc, gt, g, wg, wu, wd, tiles_per_seq, g_final=None):
    nt, d = x.shape
    tm = min(TOKEN_TILE, nt)
    args = [x, sh, sc, gt, g, wg, wu, wd]
    in_specs = [pl.BlockSpec((tm, d), lambda i: (i, 0)),
                _mod_spec(sh, tm, tiles_per_seq), _mod_spec(sc, tm, tiles_per_seq),
                _mod_spec(gt, tm, tiles_per_seq),
                _const_spec(g), _const_spec(wg), _const_spec(wu), _const_spec(wd)]
    if g_final is not None:
        args.append(g_final)
        in_specs.append(_const_spec(g_final))
    return pl.pallas_call(
        functools.partial(_ffn_kernel, g_final is not None),
        grid=(nt // tm,),
        in_specs=in_specs,
        out_specs=pl.BlockSpec((tm, d), lambda i: (i, 0)),
        out_shape=jax.ShapeDtypeStruct((nt, d), F32),
        scratch_shapes=[pltpu.VMEM((tm, d), BF16), pltpu.VMEM((tm, d), F32)],
        compiler_params=_params(1),
        name="swiglu_ffn",
    )(*args)


def _mixer_in_kernel(transposed_kv, cw, aw, x_ref, sh_ref, sc_ref, g_ref, w_ref, *rest):
    if transposed_kv:
        wkv_ref, u_ref, q_ref, k_ref, v_ref, sga_ref, sgb_ref = rest
    else:
        u_ref, q_ref, k_ref, v_ref, sga_ref, sgb_ref = rest
    d = x_ref.shape[-1]
    h = _modulated_norm(x_ref[...], g_ref[...], sh_ref[...], sc_ref[...]).astype(BF16)
    a = _dot(h, w_ref[:, 0:cw])
    g = _dot(h, w_ref[:, cw:2 * cw])
    u_ref[...] = a * jax.nn.sigmoid(g)
    o = 2 * cw
    q_ref[...] = _dot(h, w_ref[:, o:o + aw])
    if transposed_kv:
        k_ref[...] = _dot_nt(wkv_ref[0:aw, :], h)
        v_ref[...] = _dot_nt(wkv_ref[aw:2 * aw, :], h)
    else:
        k_ref[...] = _dot(h, w_ref[:, o + aw:o + 2 * aw])
        v_ref[...] = _dot(h, w_ref[:, o + 2 * aw:o + 3 * aw])
    o += 3 * aw
    sga_ref[...] = jax.nn.sigmoid(_dot(h, w_ref[:, o:o + d]))
    sgb_ref[...] = jax.nn.sigmoid(_dot(h, w_ref[:, o + d:o + 2 * d]))


def _mixer_in(x, sh, sc, g, w_in, cw, aw, tiles_per_seq, w_kv_t=None):
    nt, d = x.shape
    tm = min(TOKEN_TILE, nt)
    row = lambda i: (i, 0)
    transposed_kv = w_kv_t is not None
    if transposed_kv:
        nseq = nt // (tiles_per_seq * tm)
        kv_shape = jax.ShapeDtypeStruct((nseq, aw, tiles_per_seq * tm), F32)
        kv_spec = pl.BlockSpec((None, aw, tm), lambda i: (i // tiles_per_seq, 0, i % tiles_per_seq))
    else:
        kv_shape = jax.ShapeDtypeStruct((nt, aw), F32)
        kv_spec = pl.BlockSpec((tm, aw), row)
    out_shape = [jax.ShapeDtypeStruct((nt, cw), F32), jax.ShapeDtypeStruct((nt, aw), F32),
                 kv_shape, kv_shape] + [jax.ShapeDtypeStruct((nt, d), F32)] * 2
    out_specs = [pl.BlockSpec((tm, cw), row), pl.BlockSpec((tm, aw), row), kv_spec, kv_spec] \
        + [pl.BlockSpec((tm, d), row)] * 2
    args = [x, sh, sc, g, w_in] + ([w_kv_t] if transposed_kv else [])
    in_specs = [pl.BlockSpec((tm, d), row), _mod_spec(sh, tm, tiles_per_seq),
                _mod_spec(sc, tm, tiles_per_seq), _const_spec(g), _const_spec(w_in)] \
        + ([_const_spec(w_kv_t)] if transposed_kv else [])
    return pl.pallas_call(
        functools.partial(_mixer_in_kernel, transposed_kv, cw, aw),
        grid=(nt // tm,),
        in_specs=in_specs,
        out_specs=out_specs,
        out_shape=out_shape,
        compiler_params=_params(1),
        name="mixer_in_proj",
    )(*args)


def _ln_swish(cv, g, b):
    mu = jnp.mean(cv, axis=-1, keepdims=True)
    xc = cv - mu
    y = xc * lax.rsqrt(jnp.mean(xc * xc, axis=-1, keepdims=True) + EPS) * g + b
    return _silu(y)


HALO = 32
CONV_ROWS = 64


def _conv_prompt_kernel(conv_w, cur_ref, halo_ref, w_ref, b_ref, g_ref, bb_ref, o_ref, ext_scr):
    i = pl.program_id(1)
    tt = cur_ref.shape[0]
    n_ext = HALO + tt
    halo = halo_ref[...]
    ext_scr[0, 0:HALO, :] = jnp.where(i > 0, halo, jnp.zeros_like(halo))
    ext_scr[0, HALO:n_ext, :] = cur_ref[...]
    for s in range(1, SUBLANES):
        ext_scr[s, 0:n_ext - s, :] = ext_scr[0, s:n_ext, :]
    base = HALO - (conv_w - 1)
    for r in range(tt // CONV_ROWS):
        acc = jnp.zeros((CONV_ROWS, cur_ref.shape[1]), F32)
        for j in range(conv_w):
            off = r * CONV_ROWS + base + j
            lo = off - off % SUBLANES
            acc = acc + w_ref[j:j + 1, :] * ext_scr[off % SUBLANES, lo:lo + CONV_ROWS, :]
        y = _ln_swish(acc + b_ref[...], g_ref[...], bb_ref[...])
        o_ref[r * CONV_ROWS:(r + 1) * CONV_ROWS, :] = y.astype(o_ref.dtype)


def _conv_prompt(u, conv_w, conv_b, ln_g, ln_b):
    b, s, c = u.shape
    cw = conv_w.shape[0]
    assert cw - 1 <= HALO
    tt = min(TOKEN_TILE, s)
    hb = tt // HALO
    vec = lambda a: a.reshape(1, c)
    const = lambda bi, i: (0, 0)
    return pl.pallas_call(
        functools.partial(_conv_prompt_kernel, cw),
        grid=(b, s // tt),
        in_specs=[pl.BlockSpec((None, tt, c), lambda bi, i: (bi, i, 0)),
                  pl.BlockSpec((None, HALO, c), lambda bi, i: (bi, jnp.maximum(i * hb - 1, 0), 0)),
                  pl.BlockSpec((cw, c), const), pl.BlockSpec((1, c), const),
                  pl.BlockSpec((1, c), const), pl.BlockSpec((1, c), const)],
        out_specs=pl.BlockSpec((None, tt, c), lambda bi, i: (bi, i, 0)),
        out_shape=jax.ShapeDtypeStruct((b, s, c), BF16),
        scratch_shapes=[pltpu.VMEM((SUBLANES, HALO + tt, c), F32)],
        compiler_params=_params(2),
        name="conv_prompt",
    )(u, u, conv_w, vec(conv_b), vec(ln_g), vec(ln_b))


SEQ_CHUNK = 32


def _conv_sample_kernel(hist, st_ref, u_ref, w_ref, b_ref, g_ref, bb_ref, o_ref, ns_ref):
    t_new = u_ref.shape[0]
    row = lambda r: st_ref[r] if r < hist else u_ref[r - hist]
    for t in range(t_new):
        acc = jnp.zeros(u_ref.shape[1:], F32)
        for j in range(w_ref.shape[0]):
            acc = acc + w_ref[j:j + 1, :] * row(t + j)
        o_ref[t] = _ln_swish(acc + b_ref[...], g_ref[...], bb_ref[...])
    for r in range(hist):
        ns_ref[r] = row(r + t_new)


def _conv_sample(layer, state_t, u_t, conv_w, conv_b, ln_g, ln_b):
    _, hist, n, c = state_t.shape
    t = u_t.shape[0]
    assert conv_w.shape[0] == hist + 1
    sc = min(SEQ_CHUNK, n)
    vec = lambda a: a.reshape(1, c)
    const = lambda i: (0, 0)
    return pl.pallas_call(
        functools.partial(_conv_sample_kernel, hist),
        grid=(n // sc,),
        in_specs=[pl.BlockSpec((None, hist, sc, c), lambda i: (layer, 0, i, 0)),
                  pl.BlockSpec((t, sc, c), lambda i: (0, i, 0)),
                  pl.BlockSpec(conv_w.shape, const), pl.BlockSpec((1, c), const),
                  pl.BlockSpec((1, c), const), pl.BlockSpec((1, c), const)],
        out_specs=[pl.BlockSpec((t, sc, c), lambda i: (0, i, 0)),
                   pl.BlockSpec((hist, sc, c), lambda i: (0, i, 0))],
        out_shape=[jax.ShapeDtypeStruct((t, n, c), F32), jax.ShapeDtypeStruct((hist, n, c), F32)],
        compiler_params=_params(1),
        name="conv_sample",
    )(state_t, u_t, conv_w, vec(conv_b), vec(ln_g), vec(ln_b))


SLOTS = 16
PAD_SLOT = 3 * SLOTS
N_PAD = NEAR_BLOCKS - Q_BLOCKS
FAR_GROUP = 2
NEG_BIG = -(2.0 ** 100)


def _attn_prompt_kernel(rounds, nb, far_ref, q_ref, kt_ref, vt_ref, bias_ref, o_ref,
                        kx_scr, vx_scr, kmx_scr):
    hp = pl.program_id(1)
    i = pl.program_id(2)
    blk = MOBA_BLOCK
    tq = Q_BLOCKS * blk

    @pl.when(i == 0)
    def _():
        row = lax.broadcasted_iota(jnp.int32, (LANES, blk), 0)
        jrel = row % HEAD_DIM
        for hh in range(HEADS_PER_GROUP):
            spare = (row // HEAD_DIM) != hh
            for p in range(N_PAD):
                kx_scr[hh, p] = jnp.where(spare & (jrel == PAD_SLOT), 1.0, 0.0).astype(BF16)
            for n in range(nb):
                ind = jnp.where((jrel < PAD_SLOT) & (jrel % SLOTS == n), 1.0, 0.0)
                kx_scr[hh, n + N_PAD] = jnp.where(spare, ind,
                                                  kt_ref[:, n * blk:(n + 1) * blk]).astype(BF16)
        for p in range(N_PAD):
            vx_scr[p] = jnp.zeros((LANES, blk), BF16)
        r2 = lax.broadcasted_iota(jnp.int32, (LANES, LANES), 0)
        c2 = lax.broadcasted_iota(jnp.int32, (LANES, LANES), 1)
        kmcols = jnp.zeros((LANES, LANES), F32)
        for n in range(nb):
            vx_scr[n + N_PAD] = vt_ref[:, n * blk:(n + 1) * blk].astype(BF16)
            col = jnp.sum(kt_ref[:, n * blk:(n + 1) * blk], axis=1, keepdims=True) * (1.0 / blk)
            kmcols = jnp.where(c2 == n, col, kmcols)
        pick = jnp.where((r2 % HEAD_DIM) == c2, 1.0, 0.0)
        pick = jnp.where((r2 % HEAD_DIM) < SLOTS, pick, 0.0)
        kmx = _dot_nt(pick, kmcols, precision=lax.Precision.HIGHEST)
        kmx_scr[...] = jnp.where((r2 // HEAD_DIM) != (c2 // HEAD_DIM), kmx, 0.0)

    q = q_ref[...]
    gate_t = _dot_nt(kmx_scr[...], q, precision=lax.Precision.HIGHEST)
    n_i = lax.broadcasted_iota(jnp.int32, (SLOTS, tq), 0)
    q_i = lax.broadcasted_iota(jnp.int32, (SLOTS, tq), 1)
    own = Q_BLOCKS * i + q_i // blk
    first_near = Q_BLOCKS * i - N_PAD
    zeros = jnp.zeros((SLOTS, tq), F32)
    groups = []
    for hh in range(HEADS_PER_GROUP):
        base = (1 - hh) * HEAD_DIM
        g = jnp.where(n_i < own, gate_t[base:base + SLOTS, :], NEG)
        rank = zeros
        for m in range(nb):
            gm = g[m:m + 1, :]
            tie = jnp.where(n_i > m, 1.0, 0.0)
            rank = rank + jnp.where(gm > g, 1.0, jnp.where(gm == g, tie, 0.0))
        sel = jnp.where(n_i < own, jnp.where(rank < rounds, 1.0, 0.0), 0.0) > 0.5
        is_far = n_i < first_near
        head = hp * HEADS_PER_GROUP + hh
        far_hi = jnp.where(is_far, jnp.where(sel, far_ref[0, head], NEG_BIG), 0.0)
        far_lo = jnp.where(is_far, jnp.where(sel, far_ref[1, head], 0.0), 0.0)
        near = jnp.where(is_far, 0.0, jnp.where(n_i == own, 0.0, jnp.where(sel, 0.0, NEG_BIG)))
        pad = jnp.where(n_i == 0, NEG_BIG, 0.0)
        groups.append(jnp.concatenate([far_hi, far_lo, near, pad], axis=0))
    spare_vals = jnp.concatenate(groups[::-1], axis=0).T
    lane = lax.broadcasted_iota(jnp.int32, q.shape, 1)
    qs = q * (HEAD_DIM ** -0.5)

    qx, carry = [], []
    for hh in range(HEADS_PER_GROUP):
        qx.append(jnp.where((lane // HEAD_DIM) == hh, qs, spare_vals).astype(BF16))
        first = Q_BLOCKS * i
        ss = [_dot(qx[hh], kx_scr[hh, first + j]) + bias_ref[hh, :, j * blk:(j + 1) * blk]
              for j in range(NEAR_BLOCKS)]
        m = jnp.max(ss[0], axis=-1, keepdims=True)
        for s in ss[1:]:
            m = jnp.maximum(m, jnp.max(s, axis=-1, keepdims=True))
        ps = [jnp.exp(s - m) for s in ss]
        l = jnp.sum(ps[0], axis=-1, keepdims=True)
        for p in ps[1:]:
            l = l + jnp.sum(p, axis=-1, keepdims=True)
        acc = _dot_nt(ps[0].astype(BF16), vx_scr[first])
        for j in range(1, NEAR_BLOCKS):
            acc = acc + _dot_nt(ps[j].astype(BF16), vx_scr[first + j])
        carry.append((m, l, acc))

    def far(gi, carry):
        out = []
        first = N_PAD + gi * FAR_GROUP
        for hh in range(HEADS_PER_GROUP):
            m, l, acc = carry[hh]
            ss = [_dot(qx[hh], kx_scr[hh, first + j]) for j in range(FAR_GROUP)]
            m_new = m
            for s in ss:
                m_new = jnp.maximum(m_new, jnp.max(s, axis=-1, keepdims=True))
            alpha = jnp.exp(m - m_new)
            ps = [jnp.exp(s - m_new) for s in ss]
            l = alpha * l
            for p in ps:
                l = l + jnp.sum(p, axis=-1, keepdims=True)
            pv = _dot_nt(ps[0].astype(BF16), vx_scr[first])
            for j in range(1, FAR_GROUP):
                pv = pv + _dot_nt(ps[j].astype(BF16), vx_scr[first + j])
            out.append((m_new, l, alpha * acc + pv))
        return tuple(out)

    carry = lax.fori_loop(0, i * Q_BLOCKS // FAR_GROUP - N_PAD // FAR_GROUP, far, tuple(carry))
    out = carry[0][2] / carry[0][1]
    for hh in range(1, HEADS_PER_GROUP):
        out = jnp.where(lane >= hh * HEAD_DIM, carry[hh][2] / carry[hh][1], out)
    o_ref[...] = out.astype(o_ref.dtype)


def _attn_prompt(q, k_t, v_t, bias_tiles, far_parts):
    b, s, aw = q.shape
    nb = s // MOBA_BLOCK
    ng = aw // LANES
    tq = Q_BLOCKS * MOBA_BLOCK
    assert nb <= SLOTS and nb % Q_BLOCKS == 0 and HEADS_PER_GROUP == 2
    assert Q_BLOCKS % FAR_GROUP == 0 and N_PAD % FAR_GROUP == 0
    rounds = min(MOBA_TOPK, nb)
    grid_spec = pltpu.PrefetchScalarGridSpec(
        num_scalar_prefetch=1,
        grid=(b, ng, s // tq),
        in_specs=[pl.BlockSpec((None, tq, LANES), lambda bi, g, i, far: (bi, i, g)),
                  pl.BlockSpec((None, LANES, s), lambda bi, g, i, far: (bi, g, 0)),
                  pl.BlockSpec((None, LANES, s), lambda bi, g, i, far: (bi, g, 0)),
                  pl.BlockSpec((HEADS_PER_GROUP,) + bias_tiles.shape[1:],
                               lambda bi, g, i, far: (g, 0, 0))],
        out_specs=pl.BlockSpec((None, tq, LANES), lambda bi, g, i, far: (bi, i, g)),
        scratch_shapes=[pltpu.VMEM((HEADS_PER_GROUP, nb + N_PAD, LANES, MOBA_BLOCK), BF16),
                        pltpu.VMEM((nb + N_PAD, LANES, MOBA_BLOCK), BF16),
                        pltpu.VMEM((LANES, LANES), F32)],
    )
    return pl.pallas_call(
        functools.partial(_attn_prompt_kernel, rounds, nb),
        grid_spec=grid_spec,
        out_shape=jax.ShapeDtypeStruct((b, s, aw), BF16),
        compiler_params=_params(3),
        name="moba_prompt",
    )(far_parts, q, k_t, v_t, bias_tiles)


def _attn_sample_kernel(n_pages, page, rounds, pt_ref, q_ref, kn_ref, vn_ref, bprev_ref, bown_ref,
                        far_ref, *rest):
    kp_refs = rest[:n_pages]
    vp_refs = rest[n_pages:2 * n_pages]
    o_ref, = rest[2 * n_pages:]
    t, aw = q_ref.shape
    rows = t * N_HEADS
    ppb = MOBA_BLOCK // page
    n_past = n_pages // ppb

    r_i = lax.broadcasted_iota(jnp.int32, (rows, aw), 0)
    c_i = lax.broadcasted_iota(jnp.int32, (rows, aw), 1)
    in_head = (c_i // HEAD_DIM) == (r_i % N_HEADS)
    q = q_ref[...]
    q_rep = jnp.zeros((rows, aw), F32)
    for qi in range(t):
        q_rep = jnp.where(r_i // N_HEADS == qi, q[qi:qi + 1, :], q_rep)
    qs = (jnp.where(in_head, q_rep, 0.0) * (HEAD_DIM ** -0.5)).astype(BF16)

    raw = [jnp.concatenate([_dot(qs, kp_refs[n * ppb + j][...].astype(BF16)) for j in range(ppb)],
                           axis=-1) for n in range(n_past)]
    gsum = [jnp.sum(r, axis=-1, keepdims=True) for r in raw]
    s_blocks = []
    for n in range(n_past):
        rank = jnp.zeros((rows, 1), F32)
        for m in range(n_past):
            if m != n:
                tie = 1.0 if m < n else 0.0
                rank = rank + jnp.where(gsum[m] > gsum[n], 1.0,
                                        jnp.where(gsum[m] == gsum[n], tie, 0.0))
        s_n = raw[n] + (bprev_ref[...] if n == n_past - 1 else far_ref[...])
        s_blocks.append(jnp.where(rank < rounds, s_n, NEG))
    pad = jnp.zeros((LANES - t, aw), F32)
    kn = jnp.concatenate([kn_ref[...], pad], axis=0).astype(BF16)
    vn = jnp.concatenate([vn_ref[...], pad], axis=0).astype(BF16)
    s_own = _dot_nt(qs, kn) + bown_ref[...]

    m = jnp.max(s_own, axis=-1, keepdims=True)
    for s_n in s_blocks:
        m = jnp.maximum(m, jnp.max(s_n, axis=-1, keepdims=True))
    p_own = jnp.exp(s_own - m)
    l = jnp.sum(p_own, axis=-1, keepdims=True)
    out = _dot(p_own.astype(BF16), vn)
    for n, s_n in enumerate(s_blocks):
        p_n = jnp.exp(s_n - m)
        l = l + jnp.sum(p_n, axis=-1, keepdims=True)
        p_n = p_n.astype(BF16)
        for j in range(ppb):
            out = out + _dot_nt(p_n[:, j * page:(j + 1) * page], vp_refs[n * ppb + j][...].astype(BF16))
    out = jnp.where(in_head, out / l, 0.0)
    o_ref[...] = jnp.sum(out.reshape(t, N_HEADS, aw), axis=1)


def _attn_sample(layer, q, k_new, v_new, cache_kt, cache_vt, page_table, bprev, bown, far):
    n, t, aw = q.shape
    n_pages = page_table.shape[1]
    page = cache_kt.shape[3]
    n_past_tokens = n_pages * page
    assert page == LANES and MOBA_BLOCK % page == 0 and n_past_tokens % MOBA_BLOCK == 0
    assert t <= LANES and n_past_tokens // MOBA_BLOCK <= LANES
    n_blocks = n_past_tokens // MOBA_BLOCK + 1
    rounds = min(MOBA_TOPK, n_blocks)
    tok = pl.BlockSpec((None, t, aw), lambda i, pt: (i, 0, 0))
    const2 = lambda a: pl.BlockSpec(a.shape, lambda i, pt: (0, 0))
    page_specs = [pl.BlockSpec((None, None, aw, page), lambda i, pt, p=p: (layer, pt[i, p], 0, 0))
                  for p in range(n_pages)]
    grid_spec = pltpu.PrefetchScalarGridSpec(
        num_scalar_prefetch=1,
        grid=(n,),
        in_specs=[tok, tok, tok, const2(bprev), const2(bown), const2(far)] + page_specs + page_specs,
        out_specs=pl.BlockSpec((None, t, aw), lambda i, pt: (i, 0, 0)),
    )
    return pl.pallas_call(
        functools.partial(_attn_sample_kernel, n_pages, page, rounds),
        grid_spec=grid_spec,
        out_shape=jax.ShapeDtypeStruct((n, t, aw), F32),
        compiler_params=_params(1),
        name="moba_sample",
    )(page_table, q, k_new, v_new, bprev, bown, far, *([cache_kt] * n_pages), *([cache_vt] * n_pages))


def _mixer_out_kernel(x_ref, gt_ref, c_ref, a_ref, sga_ref, sgb_ref, wc_ref, wa_ref, wo_ref, o_ref):
    conv_y = _dot(c_ref[...].astype(BF16), wc_ref[...])
    attn_y = _dot(a_ref[...].astype(BF16), wa_ref[...])
    mixed = (sga_ref[...] * conv_y + sgb_ref[...] * attn_y).astype(BF16)
    o_ref[...] = x_ref[...] + gt_ref[...] * _dot(mixed, wo_ref[...])


def _mixer_out(x, gt, cact, att, sga, sgb, wc, wa, wo, tiles_per_seq):
    nt, d = x.shape
    tm = min(TOKEN_TILE, nt)
    row = lambda i: (i, 0)
    return pl.pallas_call(
        _mixer_out_kernel,
        grid=(nt // tm,),
        in_specs=[pl.BlockSpec((tm, d), row), _mod_spec(gt, tm, tiles_per_seq),
                  pl.BlockSpec((tm, cact.shape[1]), row), pl.BlockSpec((tm, att.shape[1]), row),
                  pl.BlockSpec((tm, d), row), pl.BlockSpec((tm, d), row),
                  _const_spec(wc), _const_spec(wa), _const_spec(wo)],
        out_specs=pl.BlockSpec((tm, d), row),
        out_shape=jax.ShapeDtypeStruct((nt, d), F32),
        compiler_params=_params(1),
        name="mixer_out",
    )(x, gt, cact, att, sga, sgb, wc, wa, wo)


def kernel(x_prompt, x_sample, c_prompt, c_sample, cache_k, cache_v, state_conv, page_table, rel_bias, w_ada, b_ada, g_norm, w_ffn_gate, w_ffn_up, w_ffn_down, w_in, conv_w, conv_b, conv_ln_g, conv_ln_b, w_conv_out, w_attn_out, w_o, g_final):
    b, s, d = x_prompt.shape
    n, t, _ = x_sample.shape
    depth = w_ada.shape[0]
    cw = conv_w.shape[2]
    aw = N_HEADS * HEAD_DIM
    d_ff = w_ffn_gate.shape[-1]

    nc = -(-(b + n) // 8) * 8
    c_all = jnp.concatenate([c_prompt, c_sample, jnp.zeros((nc - b - n, d), F32)], axis=0)
    mod = _ada(c_all, w_ada, b_ada).reshape(depth, nc, N_MOD, d)
    tiles = _bias_tiles(rel_bias)
    far_bias = rel_bias[:, N_BUCKETS - 1]
    far_hi = far_bias.astype(BF16).astype(F32)
    far_parts = jnp.stack([far_hi, far_bias - far_hi])

    rows = t * N_HEADS
    own_col = N_PAD * MOBA_BLOCK
    bprev_s = tiles[:, :t, own_col - MOBA_BLOCK:own_col].transpose(1, 0, 2).reshape(rows, MOBA_BLOCK)
    bown_s = jnp.pad(tiles[:, :t, own_col:own_col + t].transpose(1, 0, 2).reshape(rows, t),
                     ((0, 0), (0, LANES - t)), constant_values=NEG)
    far_s = jnp.broadcast_to(jnp.tile(far_bias, t)[:, None], (rows, MOBA_BLOCK))

    cache_kt = cache_k.transpose(0, 1, 3, 4, 2).reshape(cache_k.shape[:2] + (aw, cache_k.shape[2]))
    cache_vt = cache_v.transpose(0, 1, 3, 4, 2).reshape(cache_v.shape[:2] + (aw, cache_v.shape[2]))
    state_t = state_conv.transpose(0, 2, 1, 3)
    hist = state_t.shape[1]

    assert d_ff % MXU_DIM == 0
    wg, wu, wd = w_ffn_gate.astype(BF16), w_ffn_up.astype(BF16), w_ffn_down.astype(BF16)
    w_in_b = w_in.astype(BF16)
    w_kv_t = w_in[:, :, 2 * cw + aw:2 * cw + 3 * aw].transpose(0, 2, 1).astype(BF16)
    wc_b = w_conv_out.astype(BF16)
    wa_b = w_attn_out.astype(BF16)
    wo_b = w_o.astype(BF16)
    gf = g_final.reshape(1, d)

    def ffn(x, mm, l, i, tiles_per_seq, final):
        o = 6 * i
        return _ffn(x, mm[o], mm[o + 1], mm[o + 2], g_norm[l, 2 * i:2 * i + 1], wg[l, i], wu[l, i],
                    wd[l, i], tiles_per_seq, gf if final else None)

    tm_p = min(TOKEN_TILE, s)
    assert s % tm_p == 0 and s % MOBA_BLOCK == 0
    tps = s // tm_p
    x = x_prompt.reshape(b * s, d)
    kp, vp, cp = [], [], []
    for l in range(depth):
        mm = [mod[l, :b, i][:, None, :] for i in range(N_MOD)]
        x = ffn(x, mm, l, 0, tps, False)
        u, q, k_t, v_t, sga, sgb = _mixer_in(x, mm[3], mm[4], g_norm[l, 1:2], w_in_b[l], cw, aw, tps,
                                             w_kv_t[l])
        u3 = u.reshape(b, s, cw)
        cact = _conv_prompt(u3, conv_w[l], conv_b[l], conv_ln_g[l], conv_ln_b[l])
        att = _attn_prompt(q.reshape(b, s, aw), k_t, v_t, tiles, far_parts)
        x = _mixer_out(x, mm[5], cact.reshape(b * s, cw), att.reshape(b * s, aw), sga, sgb,
                       wc_b[l], wa_b[l], wo_b[l], tps)
        x = ffn(x, mm, l, 1, tps, l == depth - 1)
        kp.append(k_t)
        vp.append(v_t)
        cp.append(u3[:, s - hist:])
    y_p = x.reshape(b, s, d)
    to_cache = lambda z: jnp.stack(z).reshape(depth, b, N_HEADS, HEAD_DIM, s).transpose(0, 1, 4, 2, 3)
    k_p, v_p, c_p = to_cache(kp), to_cache(vp), jnp.stack(cp)

    x = x_sample.reshape(n * t, d)
    ks, vs, cs = [], [], []
    for l in range(depth):
        mm = [jnp.repeat(mod[l, b:b + n, i], t, axis=0) for i in range(N_MOD)]
        x = ffn(x, mm, l, 0, 1, False)
        u, q, k, v, sga, sgb = _mixer_in(x, mm[3], mm[4], g_norm[l, 1:2], w_in_b[l], cw, aw, 1)
        u_t = u.reshape(n, t, cw).transpose(1, 0, 2)
        cact_t, new_state = _conv_sample(l, state_t, u_t, conv_w[l], conv_b[l], conv_ln_g[l],
                                         conv_ln_b[l])
        att = _attn_sample(l, q.reshape(n, t, aw), k.reshape(n, t, aw), v.reshape(n, t, aw),
                           cache_kt, cache_vt, page_table, bprev_s, bown_s, far_s)
        x = _mixer_out(x, mm[5], cact_t.transpose(1, 0, 2).reshape(n * t, cw), att.reshape(n * t, aw),
                       sga, sgb, wc_b[l], wa_b[l], wo_b[l], 1)
        x = ffn(x, mm, l, 1, 1, l == depth - 1)
        ks.append(k.reshape(n, t, N_HEADS, HEAD_DIM))
        vs.append(v.reshape(n, t, N_HEADS, HEAD_DIM))
        cs.append(new_state)
    y_s = x.reshape(n, t, d)
    return (y_p, y_s, k_p, v_p, c_p, jnp.stack(ks), jnp.stack(vs),
            jnp.stack(cs).transpose(0, 2, 1, 3))
```

```python
import functools
import math

import jax
import jax.numpy as jnp
from jax import lax
from jax.experimental import pallas as pl
from jax.experimental.pallas import tpu as pltpu

N_HEADS = 8
HEAD_DIM = 64
MOBA_BLOCK = 256
MOBA_TOPK = 3
N_BUCKETS = 32
MAX_DISTANCE = 128
N_MOD = 9
HALF = 0.5
EPS = 1e-6
NEG = -1e30
LOG2E = math.log2(math.e)

LANES = 128
SUBLANES = 8
HEADS_PER_GROUP = LANES // HEAD_DIM
MXU_DIM = 256
TOKEN_TILE = 512
VMEM_LIMIT = 56 * 1024 * 1024

F32 = jnp.float32
BF16 = jnp.bfloat16


def _params(n_grid_dims):
    return pltpu.CompilerParams(dimension_semantics=("arbitrary",) * n_grid_dims,
                                vmem_limit_bytes=VMEM_LIMIT)


def _dot(a, b):
    return jnp.dot(a, b, preferred_element_type=F32)


def _dot_nt(a, b, precision=None):
    return lax.dot_general(a, b, (((1,), (1,)), ((), ())), precision=precision,
                           preferred_element_type=F32)


def _silu(x):
    return x * jax.nn.sigmoid(x)


def _modulated_norm(x, g, shift, scale):
    y = x * lax.rsqrt(jnp.mean(x * x, axis=-1, keepdims=True) + EPS) * g
    return y * (1.0 + scale) + shift


def _ada_kernel(c_ref, w_ref, b_ref, o_ref):
    a = _silu(c_ref[...]).astype(BF16)
    o_ref[...] = _dot(a, w_ref[...].astype(BF16)) + b_ref[...]


def _ada(c_all, w_ada, b_ada):
    depth, d, nd = w_ada.shape
    nc = c_all.shape[0]
    tn = nd // 6
    assert tn % LANES == 0
    return pl.pallas_call(
        _ada_kernel,
        grid=(depth, nd // tn),
        in_specs=[pl.BlockSpec((nc, d), lambda l, j: (0, 0)),
                  pl.BlockSpec((None, d, tn), lambda l, j: (l, 0, j)),
                  pl.BlockSpec((None, 1, tn), lambda l, j: (l, 0, j))],
        out_specs=pl.BlockSpec((None, nc, tn), lambda l, j: (l, 0, j)),
        out_shape=jax.ShapeDtypeStruct((depth, nc, nd), F32),
        compiler_params=_params(2),
        name="ada_mod",
    )(c_all, w_ada, b_ada.reshape(depth, 1, nd))


Q_BLOCKS = 2
NEAR_BLOCKS = 4


def _bias_kernel(scale, rel_ref, o_ref):
    h = pl.program_id(0)
    shape = (NEAR_BLOCKS * MOBA_BLOCK, Q_BLOCKS * MOBA_BLOCK)
    key = lax.broadcasted_iota(jnp.int32, shape, 0)
    qry = lax.broadcasted_iota(jnp.int32, shape, 1)
    d = qry - key + (NEAR_BLOCKS - Q_BLOCKS) * MOBA_BLOCK
    n = jnp.maximum(d, 0)
    max_exact = N_BUCKETS // 2
    nf = jnp.maximum(n, 1).astype(F32)
    large = max_exact + (jnp.log(nf / max_exact) / math.log(MAX_DISTANCE / max_exact)
                         * (N_BUCKETS - max_exact)).astype(jnp.int32)
    large = jnp.minimum(large, N_BUCKETS - 1)
    bucket = jnp.where(n < max_exact, n, large)
    bias = jnp.zeros(shape, F32)
    for b in range(N_BUCKETS):
        bias = jnp.where(bucket == b, rel_ref[h, b], bias)
    o_ref[...] = jnp.where(d < 0, NEG, bias * scale)


def _bias_tiles(rel_bias, scale):
    shape = (NEAR_BLOCKS * MOBA_BLOCK, Q_BLOCKS * MOBA_BLOCK)
    return pl.pallas_call(
        functools.partial(_bias_kernel, scale),
        grid=(N_HEADS,),
        in_specs=[pl.BlockSpec(memory_space=pltpu.SMEM)],
        out_specs=pl.BlockSpec((None,) + shape, lambda h: (h, 0, 0)),
        out_shape=jax.ShapeDtypeStruct((N_HEADS,) + shape, F32),
        compiler_params=_params(1),
        name="t5_bias_tiles",
    )(rel_bias)


def _mod_spec(m, tm, tiles_per_seq):
    if m.ndim == 3:
        return pl.BlockSpec((None, 1, m.shape[-1]), lambda i: (i // tiles_per_seq, 0, 0))
    return pl.BlockSpec((tm, m.shape[-1]), lambda i: (i, 0))


def _const_spec(a):
    zeros = (0,) * a.ndim
    return pl.BlockSpec(a.shape, lambda i: zeros)


def _ffn_kernel(final_norm, x_ref, sh_ref, sc_ref, gt_ref, g_ref, wg_ref, wu_ref, wd_ref, *rest):
    if final_norm:
        gf_ref, o_ref, h_scr, acc_scr = rest
    else:
        o_ref, h_scr, acc_scr = rest
    x = x_ref[...]
    h_scr[...] = _modulated_norm(x, g_ref[...], sh_ref[...], sc_ref[...]).astype(BF16)
    acc_scr[...] = jnp.zeros_like(acc_scr)

    def chunk(c, carry):
        cols = pl.ds(pl.multiple_of(c * MXU_DIM, MXU_DIM), MXU_DIM)
        h = h_scr[...]
        g = _dot(h, wg_ref[:, cols])
        u = _dot(h, wu_ref[:, cols])
        acc_scr[...] += _dot((_silu(g) * u).astype(BF16), wd_ref[cols, :])
        return carry

    lax.fori_loop(0, wg_ref.shape[1] // MXU_DIM, chunk, 0)
    y = x + (HALF * gt_ref[...]) * acc_scr[...]
    if final_norm:
        y = y * lax.rsqrt(jnp.mean(y * y, axis=-1, keepdims=True) + EPS) * gf_ref[...]
    o_ref[...] = y


def _ffn(x, sh, sc, gt, g, wg, wu, wd, tiles_per_seq, g_final=None):
    nt, d = x.shape
    tm = min(TOKEN_TILE, nt)
    args = [x, sh, sc, gt, g, wg, wu, wd]
    in_specs = [pl.BlockSpec((tm, d), lambda i: (i, 0)),
                _mod_spec(sh, tm, tiles_per_seq), _mod_spec(sc, tm, tiles_per_seq),
                _mod_spec(gt, tm, tiles_per_seq),
                _const_spec(g), _const_spec(wg), _const_spec(wu), _const_spec(wd)]
    if g_final is not None:
        args.append(g_final)
        in_specs.append(_const_spec(g_final))
    return pl.pallas_call(
        functools.partial(_ffn_kernel, g_final is not None),
        grid=(nt // tm,),
        in_specs=in_specs,
        out_specs=pl.BlockSpec((tm, d), lambda i: (i, 0)),
        out_shape=jax.ShapeDtypeStruct((nt, d), F32),
        scratch_shapes=[pltpu.VMEM((tm, d), BF16), pltpu.VMEM((tm, d), F32)],
        compiler_params=_params(1),
        name="swiglu_ffn",
    )(*args)


def _mixer_in_kernel(transposed_qkv, cw, aw, x_ref, sh_ref, sc_ref, g_ref, w_ref, *rest):
    if transposed_qkv:
        wqkv_ref, u_ref, q_ref, k_ref, v_ref, sga_ref, sgb_ref = rest
    else:
        u_ref, q_ref, k_ref, v_ref, sga_ref, sgb_ref = rest
    d = x_ref.shape[-1]
    h = _modulated_norm(x_ref[...], g_ref[...], sh_ref[...], sc_ref[...]).astype(BF16)
    a = _dot(h, w_ref[:, 0:cw])
    g = _dot(h, w_ref[:, cw:2 * cw])
    u_ref[...] = a * jax.nn.sigmoid(g)
    o = 2 * cw
    if transposed_qkv:
        q_ref[...] = _dot_nt(wqkv_ref[0:aw, :], h)
        k_ref[...] = _dot_nt(wqkv_ref[aw:2 * aw, :], h)
        v_ref[...] = _dot_nt(wqkv_ref[2 * aw:3 * aw, :], h)
    else:
        q_ref[...] = _dot(h, w_ref[:, o:o + aw])
        k_ref[...] = _dot(h, w_ref[:, o + aw:o + 2 * aw])
        v_ref[...] = _dot(h, w_ref[:, o + 2 * aw:o + 3 * aw])
    o += 3 * aw
    sga_ref[...] = jax.nn.sigmoid(_dot(h, w_ref[:, o:o + d]))
    sgb_ref[...] = jax.nn.sigmoid(_dot(h, w_ref[:, o + d:o + 2 * d]))


def _mixer_in(x, sh, sc, g, w_in, cw, aw, tiles_per_seq, w_qkv_t=None):
    nt, d = x.shape
    tm = min(TOKEN_TILE, nt)
    row = lambda i: (i, 0)
    transposed_qkv = w_qkv_t is not None
    if transposed_qkv:
        nseq = nt // (tiles_per_seq * tm)
        kv_shape = jax.ShapeDtypeStruct((nseq, aw, tiles_per_seq * tm), F32)
        kv_spec = pl.BlockSpec((None, aw, tm), lambda i: (i // tiles_per_seq, 0, i % tiles_per_seq))
    else:
        kv_shape = jax.ShapeDtypeStruct((nt, aw), F32)
        kv_spec = pl.BlockSpec((tm, aw), row)
    out_shape = [jax.ShapeDtypeStruct((nt, cw), F32), kv_shape, kv_shape, kv_shape] \
        + [jax.ShapeDtypeStruct((nt, d), F32)] * 2
    out_specs = [pl.BlockSpec((tm, cw), row), kv_spec, kv_spec, kv_spec] \
        + [pl.BlockSpec((tm, d), row)] * 2
    args = [x, sh, sc, g, w_in] + ([w_qkv_t] if transposed_qkv else [])
    in_specs = [pl.BlockSpec((tm, d), row), _mod_spec(sh, tm, tiles_per_seq),
                _mod_spec(sc, tm, tiles_per_seq), _const_spec(g), _const_spec(w_in)] \
        + ([_const_spec(w_qkv_t)] if transposed_qkv else [])
    return pl.pallas_call(
        functools.partial(_mixer_in_kernel, transposed_qkv, cw, aw),
        grid=(nt // tm,),
        in_specs=in_specs,
        out_specs=out_specs,
        out_shape=out_shape,
        compiler_params=_params(1),
        name="mixer_in_proj",
    )(*args)


def _ln_swish(cv, g, b):
    mu = jnp.mean(cv, axis=-1, keepdims=True)
    xc = cv - mu
    y = xc * lax.rsqrt(jnp.mean(xc * xc, axis=-1, keepdims=True) + EPS) * g + b
    return _silu(y)


HALO = 32
CONV_ROWS = 64


def _conv_prompt_kernel(conv_w, cur_ref, halo_ref, w_ref, b_ref, g_ref, bb_ref, o_ref, ext_scr):
    i = pl.program_id(1)
    tt = cur_ref.shape[0]
    n_ext = HALO + tt
    halo = halo_ref[...]
    ext_scr[0, 0:HALO, :] = jnp.where(i > 0, halo, jnp.zeros_like(halo))
    ext_scr[0, HALO:n_ext, :] = cur_ref[...]
    for s in range(1, SUBLANES):
        ext_scr[s, 0:n_ext - s, :] = ext_scr[0, s:n_ext, :]
    base = HALO - (conv_w - 1)
    for r in range(tt // CONV_ROWS):
        acc = jnp.zeros((CONV_ROWS, cur_ref.shape[1]), F32)
        for j in range(conv_w):
            off = r * CONV_ROWS + base + j
            lo = off - off % SUBLANES
            acc = acc + w_ref[j:j + 1, :] * ext_scr[off % SUBLANES, lo:lo + CONV_ROWS, :]
        y = _ln_swish(acc + b_ref[...], g_ref[...], bb_ref[...])
        o_ref[r * CONV_ROWS:(r + 1) * CONV_ROWS, :] = y.astype(o_ref.dtype)


def _conv_prompt(u, conv_w, conv_b, ln_g, ln_b):
    b, s, c = u.shape
    cw = conv_w.shape[0]
    assert cw - 1 <= HALO
    tt = min(TOKEN_TILE, s)
    hb = tt // HALO
    vec = lambda a: a.reshape(1, c)
    const = lambda bi, i: (0, 0)
    return pl.pallas_call(
        functools.partial(_conv_prompt_kernel, cw),
        grid=(b, s // tt),
        in_specs=[pl.BlockSpec((None, tt, c), lambda bi, i: (bi, i, 0)),
                  pl.BlockSpec((None, HALO, c), lambda bi, i: (bi, jnp.maximum(i * hb - 1, 0), 0)),
                  pl.BlockSpec((cw, c), const), pl.BlockSpec((1, c), const),
                  pl.BlockSpec((1, c), const), pl.BlockSpec((1, c), const)],
        out_specs=pl.BlockSpec((None, tt, c), lambda bi, i: (bi, i, 0)),
        out_shape=jax.ShapeDtypeStruct((b, s, c), BF16),
        scratch_shapes=[pltpu.VMEM((SUBLANES, HALO + tt, c), F32)],
        compiler_params=_params(2),
        name="conv_prompt",
    )(u, u, conv_w, vec(conv_b), vec(ln_g), vec(ln_b))


SEQ_CHUNK = 32


def _conv_sample_kernel(hist, st_ref, u_ref, w_ref, b_ref, g_ref, bb_ref, o_ref, ns_ref):
    t_new = u_ref.shape[0]
    row = lambda r: st_ref[r] if r < hist else u_ref[r - hist]
    for t in range(t_new):
        acc = jnp.zeros(u_ref.shape[1:], F32)
        for j in range(w_ref.shape[0]):
            acc = acc + w_ref[j:j + 1, :] * row(t + j)
        o_ref[t] = _ln_swish(acc + b_ref[...], g_ref[...], bb_ref[...])
    for r in range(hist):
        ns_ref[r] = row(r + t_new)


def _conv_sample(layer, state_t, u_t, conv_w, conv_b, ln_g, ln_b):
    _, hist, n, c = state_t.shape
    t = u_t.shape[0]
    assert conv_w.shape[0] == hist + 1
    sc = min(SEQ_CHUNK, n)
    vec = lambda a: a.reshape(1, c)
    const = lambda i: (0, 0)
    return pl.pallas_call(
        functools.partial(_conv_sample_kernel, hist),
        grid=(n // sc,),
        in_specs=[pl.BlockSpec((None, hist, sc, c), lambda i: (layer, 0, i, 0)),
                  pl.BlockSpec((t, sc, c), lambda i: (0, i, 0)),
                  pl.BlockSpec(conv_w.shape, const), pl.BlockSpec((1, c), const),
                  pl.BlockSpec((1, c), const), pl.BlockSpec((1, c), const)],
        out_specs=[pl.BlockSpec((t, sc, c), lambda i: (0, i, 0)),
                   pl.BlockSpec((hist, sc, c), lambda i: (0, i, 0))],
        out_shape=[jax.ShapeDtypeStruct((t, n, c), F32), jax.ShapeDtypeStruct((hist, n, c), F32)],
        compiler_params=_params(1),
        name="conv_sample",
    )(state_t, u_t, conv_w, vec(conv_b), vec(ln_g), vec(ln_b))


SLOTS = 16
PAD_SLOT = 3 * SLOTS
N_PAD = NEAR_BLOCKS - Q_BLOCKS
FAR_GROUP = 2
NEG_BIG = -(2.0 ** 100)


def _attn_prompt_kernel(rounds, nb, far_ref, qt_ref, kt_ref, vt_ref, bias_ref, o_ref,
                        kx_scr, vx_scr, kmx_scr):
    hp = pl.program_id(1)
    i = pl.program_id(2)
    blk = MOBA_BLOCK
    tq = Q_BLOCKS * blk

    @pl.when(i == 0)
    def _():
        lane = lax.broadcasted_iota(jnp.int32, (blk, LANES), 1)
        jrel = lane % HEAD_DIM
        r2 = lax.broadcasted_iota(jnp.int32, (LANES, LANES), 0)
        c2 = lax.broadcasted_iota(jnp.int32, (LANES, LANES), 1)
        kmcols = jnp.zeros((LANES, LANES), F32)
        for p in range(N_PAD):
            vx_scr[p] = jnp.zeros((LANES, blk), BF16)
            for hh in range(HEADS_PER_GROUP):
                spare = (lane // HEAD_DIM) != hh
                kx_scr[hh, p] = jnp.where(spare & (jrel == PAD_SLOT), 1.0, 0.0).astype(BF16)
        for n in range(nb):
            kt_blk = kt_ref[:, n * blk:(n + 1) * blk]
            vx_scr[n + N_PAD] = vt_ref[:, n * blk:(n + 1) * blk].astype(BF16)
            kmcols = jnp.where(c2 == n, jnp.sum(kt_blk, axis=1, keepdims=True) * (1.0 / blk), kmcols)
            k_blk = kt_blk.T
            ind = jnp.where((jrel < PAD_SLOT) & (jrel % SLOTS == n), 1.0, 0.0)
            for hh in range(HEADS_PER_GROUP):
                spare = (lane // HEAD_DIM) != hh
                kx_scr[hh, n + N_PAD] = jnp.where(spare, ind, k_blk).astype(BF16)
        pick = jnp.where((r2 % HEAD_DIM) == c2, 1.0, 0.0)
        pick = jnp.where((r2 % HEAD_DIM) < SLOTS, pick, 0.0)
        kmx = _dot_nt(pick, kmcols, precision=lax.Precision.HIGHEST)
        kmx_scr[...] = jnp.where((r2 // HEAD_DIM) != (c2 // HEAD_DIM), kmx, 0.0)

    qt = qt_ref[...]
    gate_t = jnp.dot(kmx_scr[...], qt, precision=lax.Precision.HIGHEST,
                     preferred_element_type=F32)
    n_i = lax.broadcasted_iota(jnp.int32, (SLOTS, tq), 0)
    q_i = lax.broadcasted_iota(jnp.int32, (SLOTS, tq), 1)
    own = Q_BLOCKS * i + q_i // blk
    first_near = Q_BLOCKS * i - N_PAD
    zeros = jnp.zeros((SLOTS, tq), F32)
    groups = []
    for hh in range(HEADS_PER_GROUP):
        base = (1 - hh) * HEAD_DIM
        g = jnp.where(n_i < own, gate_t[base:base + SLOTS, :], NEG)
        rank = zeros
        for m in range(nb):
            gm = g[m:m + 1, :]
            tie = jnp.where(n_i > m, 1.0, 0.0)
            rank = rank + jnp.where(gm > g, 1.0, jnp.where(gm == g, tie, 0.0))
        sel = jnp.where(n_i < own, jnp.where(rank < rounds, 1.0, 0.0), 0.0) > 0.5
        is_far = n_i < first_near
        head = hp * HEADS_PER_GROUP + hh
        far_hi = jnp.where(is_far, jnp.where(sel, far_ref[0, head], NEG_BIG), 0.0)
        far_lo = jnp.where(is_far, jnp.where(sel, far_ref[1, head], 0.0), 0.0)
        near = jnp.where(is_far, 0.0, jnp.where(n_i == own, 0.0, jnp.where(sel, 0.0, NEG_BIG)))
        pad = jnp.where(n_i == 0, NEG_BIG, 0.0)
        groups.append(jnp.concatenate([far_hi, far_lo, near, pad], axis=0))
    spare_vals = jnp.concatenate(groups[::-1], axis=0)
    chan = lax.broadcasted_iota(jnp.int32, qt.shape, 0)
    qs = qt * (HEAD_DIM ** -0.5 * LOG2E)

    def attend(hh, qx, first, n_blocks, bias):
        ss = []
        for j in range(n_blocks):
            s = _dot(kx_scr[hh, first + j], qx)
            if bias is not None:
                s = s + bias_ref[hh, j * blk:(j + 1) * blk, :]
            ss.append(s)
        m = jnp.max(ss[0], axis=0, keepdims=True)
        for s in ss[1:]:
            m = jnp.maximum(m, jnp.max(s, axis=0, keepdims=True))
        return ss, m

    def weigh(hh, ss, m, first):
        rows = slice(hh * HEAD_DIM, (hh + 1) * HEAD_DIM)
        l, acc = None, None
        for j, s in enumerate(ss):
            p = jnp.exp2(s - m)
            lj = jnp.sum(p, axis=0, keepdims=True)
            aj = _dot(vx_scr[first + j][rows, :], p.astype(BF16))
            l = lj if l is None else l + lj
            acc = aj if acc is None else acc + aj
        return l, acc

    heads = range(HEADS_PER_GROUP)
    qx = [jnp.where((chan // HEAD_DIM) == hh, qs, spare_vals).astype(BF16) for hh in heads]
    first = Q_BLOCKS * i
    scored = [attend(hh, qx[hh], first, NEAR_BLOCKS, True) for hh in heads]
    carry = [(scored[hh][1],) + weigh(hh, scored[hh][0], scored[hh][1], first) for hh in heads]

    def far(gi, carry):
        out = []
        first = N_PAD + gi * FAR_GROUP
        scored = [attend(hh, qx[hh], first, FAR_GROUP, None) for hh in heads]
        for hh in heads:
            m, l, acc = carry[hh]
            ss, m_blk = scored[hh]
            m_new = jnp.maximum(m, m_blk)
            alpha = jnp.exp2(m - m_new)
            l_blk, acc_blk = weigh(hh, ss, m_new, first)
            out.append((m_new, alpha * l + l_blk, alpha * acc + acc_blk))
        return tuple(out)

    carry = lax.fori_loop(0, i * Q_BLOCKS // FAR_GROUP - N_PAD // FAR_GROUP, far, tuple(carry))
    out_t = jnp.concatenate([acc / l for _, l, acc in carry], axis=0)
    o_ref[...] = out_t.T.astype(o_ref.dtype)


def _attn_prompt(q_t, k_t, v_t, bias_tiles, far_parts):
    b, aw, s = q_t.shape
    nb = s // MOBA_BLOCK
    ng = aw // LANES
    tq = Q_BLOCKS * MOBA_BLOCK
    assert nb <= SLOTS and nb % Q_BLOCKS == 0 and HEADS_PER_GROUP == 2
    assert Q_BLOCKS % FAR_GROUP == 0 and N_PAD % FAR_GROUP == 0
    rounds = min(MOBA_TOPK, nb)
    grid_spec = pltpu.PrefetchScalarGridSpec(
        num_scalar_prefetch=1,
        grid=(b, ng, s // tq),
        in_specs=[pl.BlockSpec((None, LANES, tq), lambda bi, g, i, far: (bi, g, i)),
                  pl.BlockSpec((None, LANES, s), lambda bi, g, i, far: (bi, g, 0)),
                  pl.BlockSpec((None, LANES, s), lambda bi, g, i, far: (bi, g, 0)),
                  pl.BlockSpec((HEADS_PER_GROUP,) + bias_tiles.shape[1:],
                               lambda bi, g, i, far: (g, 0, 0))],
        out_specs=pl.BlockSpec((None, tq, LANES), lambda bi, g, i, far: (bi, i, g)),
        scratch_shapes=[pltpu.VMEM((HEADS_PER_GROUP, nb + N_PAD, MOBA_BLOCK, LANES), BF16),
                        pltpu.VMEM((nb + N_PAD, LANES, MOBA_BLOCK), BF16),
                        pltpu.VMEM((LANES, LANES), F32)],
    )
    return pl.pallas_call(
        functools.partial(_attn_prompt_kernel, rounds, nb),
        grid_spec=grid_spec,
        out_shape=jax.ShapeDtypeStruct((b, s, aw), BF16),
        compiler_params=_params(3),
        name="moba_prompt",
    )(far_parts, q_t, k_t, v_t, bias_tiles)


def _attn_sample_kernel(n_pages, page, rounds, pt_ref, q_ref, kn_ref, vn_ref, bprev_ref, bown_ref,
                        far_ref, *rest):
    kp_refs = rest[:n_pages]
    vp_refs = rest[n_pages:2 * n_pages]
    o_ref, = rest[2 * n_pages:]
    t, aw = q_ref.shape
    rows = t * N_HEADS
    ppb = MOBA_BLOCK // page
    n_past = n_pages // ppb

    r_i = lax.broadcasted_iota(jnp.int32, (rows, aw), 0)
    c_i = lax.broadcasted_iota(jnp.int32, (rows, aw), 1)
    in_head = (c_i // HEAD_DIM) == (r_i % N_HEADS)
    q = q_ref[...]
    q_rep = jnp.zeros((rows, aw), F32)
    for qi in range(t):
        q_rep = jnp.where(r_i // N_HEADS == qi, q[qi:qi + 1, :], q_rep)
    qs = (jnp.where(in_head, q_rep, 0.0) * (HEAD_DIM ** -0.5)).astype(BF16)

    raw = [jnp.concatenate([_dot(qs, kp_refs[n * ppb + j][...].astype(BF16)) for j in range(ppb)],
                           axis=-1) for n in range(n_past)]
    gsum = [jnp.sum(r, axis=-1, keepdims=True) for r in raw]
    s_blocks = []
    for n in range(n_past):
        rank = jnp.zeros((rows, 1), F32)
        for m in range(n_past):
            if m != n:
                tie = 1.0 if m < n else 0.0
                rank = rank + jnp.where(gsum[m] > gsum[n], 1.0,
                                        jnp.where(gsum[m] == gsum[n], tie, 0.0))
        s_n = raw[n] + (bprev_ref[...] if n == n_past - 1 else far_ref[...])
        s_blocks.append(jnp.where(rank < rounds, s_n, NEG))
    pad = jnp.zeros((LANES - t, aw), F32)
    kn = jnp.concatenate([kn_ref[...], pad], axis=0).astype(BF16)
    vn = jnp.concatenate([vn_ref[...], pad], axis=0).astype(BF16)
    s_own = _dot_nt(qs, kn) + bown_ref[...]

    m = jnp.max(s_own, axis=-1, keepdims=True)
    for s_n in s_blocks:
        m = jnp.maximum(m, jnp.max(s_n, axis=-1, keepdims=True))
    p_own = jnp.exp(s_own - m)
    l = jnp.sum(p_own, axis=-1, keepdims=True)
    out = _dot(p_own.astype(BF16), vn)
    for n, s_n in enumerate(s_blocks):
        p_n = jnp.exp(s_n - m)
        l = l + jnp.sum(p_n, axis=-1, keepdims=True)
        p_n = p_n.astype(BF16)
        for j in range(ppb):
            out = out + _dot_nt(p_n[:, j * page:(j + 1) * page], vp_refs[n * ppb + j][...].astype(BF16))
    out = jnp.where(in_head, out / l, 0.0)
    o_ref[...] = jnp.sum(out.reshape(t, N_HEADS, aw), axis=1)


def _attn_sample(layer, q, k_new, v_new, cache_kt, cache_vt, page_table, bprev, bown, far):
    n, t, aw = q.shape
    n_pages = page_table.shape[1]
    page = cache_kt.shape[3]
    n_past_tokens = n_pages * page
    assert page == LANES and MOBA_BLOCK % page == 0 and n_past_tokens % MOBA_BLOCK == 0
    assert t <= LANES and n_past_tokens // MOBA_BLOCK <= LANES
    n_blocks = n_past_tokens // MOBA_BLOCK + 1
    rounds = min(MOBA_TOPK, n_blocks)
    tok = pl.BlockSpec((None, t, aw), lambda i, pt: (i, 0, 0))
    const2 = lambda a: pl.BlockSpec(a.shape, lambda i, pt: (0, 0))
    page_specs = [pl.BlockSpec((None, None, aw, page), lambda i, pt, p=p: (layer, pt[i, p], 0, 0))
                  for p in range(n_pages)]
    grid_spec = pltpu.PrefetchScalarGridSpec(
        num_scalar_prefetch=1,
        grid=(n,),
        in_specs=[tok, tok, tok, const2(bprev), const2(bown), const2(far)] + page_specs + page_specs,
        out_specs=pl.BlockSpec((None, t, aw), lambda i, pt: (i, 0, 0)),
    )
    return pl.pallas_call(
        functools.partial(_attn_sample_kernel, n_pages, page, rounds),
        grid_spec=grid_spec,
        out_shape=jax.ShapeDtypeStruct((n, t, aw), F32),
        compiler_params=_params(1),
        name="moba_sample",
    )(page_table, q, k_new, v_new, bprev, bown, far, *([cache_kt] * n_pages), *([cache_vt] * n_pages))


def _mixer_out_kernel(x_ref, gt_ref, c_ref, a_ref, sga_ref, sgb_ref, wc_ref, wa_ref, wo_ref, o_ref):
    conv_y = _dot(c_ref[...].astype(BF16), wc_ref[...])
    attn_y = _dot(a_ref[...].astype(BF16), wa_ref[...])
    mixed = (sga_ref[...] * conv_y + sgb_ref[...] * attn_y).astype(BF16)
    o_ref[...] = x_ref[...] + gt_ref[...] * _dot(mixed, wo_ref[...])


def _mixer_out(x, gt, cact, att, sga, sgb, wc, wa, wo, tiles_per_seq):
    nt, d = x.shape
    tm = min(TOKEN_TILE, nt)
    row = lambda i: (i, 0)
    return pl.pallas_call(
        _mixer_out_kernel,
        grid=(nt // tm,),
        in_specs=[pl.BlockSpec((tm, d), row), _mod_spec(gt, tm, tiles_per_seq),
                  pl.BlockSpec((tm, cact.shape[1]), row), pl.BlockSpec((tm, att.shape[1]), row),
                  pl.BlockSpec((tm, d), row), pl.BlockSpec((tm, d), row),
                  _const_spec(wc), _const_spec(wa), _const_spec(wo)],
        out_specs=pl.BlockSpec((tm, d), row),
        out_shape=jax.ShapeDtypeStruct((nt, d), F32),
        compiler_params=_params(1),
        name="mixer_out",
    )(x, gt, cact, att, sga, sgb, wc, wa, wo)


def kernel(x_prompt, x_sample, c_prompt, c_sample, cache_k, cache_v, state_conv, page_table, rel_bias, w_ada, b_ada, g_norm, w_ffn_gate, w_ffn_up, w_ffn_down, w_in, conv_w, conv_b, conv_ln_g, conv_ln_b, w_conv_out, w_attn_out, w_o, g_final):
    b, s, d = x_prompt.shape
    n, t, _ = x_sample.shape
    depth = w_ada.shape[0]
    cw = conv_w.shape[2]
    aw = N_HEADS * HEAD_DIM
    d_ff = w_ffn_gate.shape[-1]

    nc = -(-(b + n) // 8) * 8
    c_all = jnp.concatenate([c_prompt, c_sample, jnp.zeros((nc - b - n, d), F32)], axis=0)
    mod = _ada(c_all, w_ada, b_ada).reshape(depth, nc, N_MOD, d)
    tiles = _bias_tiles(rel_bias, 1.0)
    tiles_log2 = _bias_tiles(rel_bias, LOG2E)
    far_bias = rel_bias[:, N_BUCKETS - 1]
    far_log2 = far_bias * LOG2E
    far_hi = far_log2.astype(BF16).astype(F32)
    far_parts = jnp.stack([far_hi, far_log2 - far_hi])

    rows = t * N_HEADS
    own_key = N_PAD * MOBA_BLOCK
    bprev_s = tiles[:, own_key - MOBA_BLOCK:own_key, :t].transpose(2, 0, 1).reshape(rows, MOBA_BLOCK)
    bown_s = jnp.pad(tiles[:, own_key:own_key + t, :t].transpose(2, 0, 1).reshape(rows, t),
                     ((0, 0), (0, LANES - t)), constant_values=NEG)
    far_s = jnp.broadcast_to(jnp.tile(far_bias, t)[:, None], (rows, MOBA_BLOCK))

    cache_kt = cache_k.transpose(0, 1, 3, 4, 2).reshape(cache_k.shape[:2] + (aw, cache_k.shape[2]))
    cache_vt = cache_v.transpose(0, 1, 3, 4, 2).reshape(cache_v.shape[:2] + (aw, cache_v.shape[2]))
    state_t = state_conv.transpose(0, 2, 1, 3)
    hist = state_t.shape[1]

    assert d_ff % MXU_DIM == 0
    wg, wu, wd = w_ffn_gate.astype(BF16), w_ffn_up.astype(BF16), w_ffn_down.astype(BF16)
    w_in_b = w_in.astype(BF16)
    w_qkv_t = w_in[:, :, 2 * cw:2 * cw + 3 * aw].transpose(0, 2, 1).astype(BF16)
    wc_b = w_conv_out.astype(BF16)
    wa_b = w_attn_out.astype(BF16)
    wo_b = w_o.astype(BF16)
    gf = g_final.reshape(1, d)

    def ffn(x, mm, l, i, tiles_per_seq, final):
        o = 6 * i
        return _ffn(x, mm[o], mm[o + 1], mm[o + 2], g_norm[l, 2 * i:2 * i + 1], wg[l, i], wu[l, i],
                    wd[l, i], tiles_per_seq, gf if final else None)

    tm_p = min(TOKEN_TILE, s)
    assert s % tm_p == 0 and s % MOBA_BLOCK == 0
    tps = s // tm_p
    x = x_prompt.reshape(b * s, d)
    kp, vp, cp = [], [], []
    for l in range(depth):
        mm = [mod[l, :b, i][:, None, :] for i in range(N_MOD)]
        x = ffn(x, mm, l, 0, tps, False)
        u, q_t, k_t, v_t, sga, sgb = _mixer_in(x, mm[3], mm[4], g_norm[l, 1:2], w_in_b[l], cw, aw,
                                               tps, w_qkv_t[l])
        u3 = u.reshape(b, s, cw)
        cact = _conv_prompt(u3, conv_w[l], conv_b[l], conv_ln_g[l], conv_ln_b[l])
        att = _attn_prompt(q_t, k_t, v_t, tiles_log2, far_parts)
        x = _mixer_out(x, mm[5], cact.reshape(b * s, cw), att.reshape(b * s, aw), sga, sgb,
                       wc_b[l], wa_b[l], wo_b[l], tps)
        x = ffn(x, mm, l, 1, tps, l == depth - 1)
        kp.append(k_t)
        vp.append(v_t)
        cp.append(u3[:, s - hist:])
    y_p = x.reshape(b, s, d)
    to_cache = lambda z: jnp.stack(z).reshape(depth, b, N_HEADS, HEAD_DIM, s).transpose(0, 1, 4, 2, 3)
    k_p, v_p, c_p = to_cache(kp), to_cache(vp), jnp.stack(cp)

    x = x_sample.reshape(n * t, d)
    ks, vs, cs = [], [], []
    for l in range(depth):
        mm = [jnp.repeat(mod[l, b:b + n, i], t, axis=0) for i in range(N_MOD)]
        x = ffn(x, mm, l, 0, 1, False)
        u, q, k, v, sga, sgb = _mixer_in(x, mm[3], mm[4], g_norm[l, 1:2], w_in_b[l], cw, aw, 1)
        u_t = u.reshape(n, t, cw).transpose(1, 0, 2)
        cact_t, new_state = _conv_sample(l, state_t, u_t, conv_w[l], conv_b[l], conv_ln_g[l],
                                         conv_ln_b[l])
        att = _attn_sample(l, q.reshape(n, t, aw), k.reshape(n, t, aw), v.reshape(n, t, aw),
                           cache_kt, cache_vt, page_table, bprev_s, bown_s, far_s)
        x = _mixer_out(x, mm[5], cact_t.transpose(1, 0, 2).reshape(n * t, cw), att.reshape(n * t, aw),
                       sga, sgb, wc_b[l], wa_b[l], wo_b[l], 1)
        x = ffn(x, mm, l, 1, 1, l == depth - 1)
        ks.append(k.reshape(n, t, N_HEADS, HEAD_DIM))
        vs.append(v.reshape(n, t, N_HEADS, HEAD_DIM))
        cs.append(new_state)
    y_s = x.reshape(n, t, d)
    return (y_p, y_s, k_p, v_p, c_p, jnp.stack(ks), jnp.stack(vs),
            jnp.stack(cs).transpose(0, 2, 1, 3))
```

```python
import functools
import math

import jax
import jax.numpy as jnp
from jax import lax
from jax.experimental import pallas as pl
from jax.experimental.pallas import tpu as pltpu

N_HEADS = 8
HEAD_DIM = 64
MOBA_BLOCK = 256
MOBA_TOPK = 3
N_BUCKETS = 32
MAX_DISTANCE = 128
N_MOD = 9
HALF = 0.5
EPS = 1e-6
NEG = -1e30
LOG2E = math.log2(math.e)

LANES = 128
SUBLANES = 8
HEADS_PER_GROUP = LANES // HEAD_DIM
MXU_DIM = 256
TOKEN_TILE = 512
VMEM_LIMIT = 56 * 1024 * 1024

F32 = jnp.float32
BF16 = jnp.bfloat16


def _params(n_grid_dims):
    return pltpu.CompilerParams(dimension_semantics=("arbitrary",) * n_grid_dims,
                                vmem_limit_bytes=VMEM_LIMIT)


def _dot(a, b):
    return jnp.dot(a, b, preferred_element_type=F32)


def _dot_nt(a, b, precision=None):
    return lax.dot_general(a, b, (((1,), (1,)), ((), ())), precision=precision,
                           preferred_element_type=F32)


def _silu(x):
    return x * jax.nn.sigmoid(x)


def _modulated_norm(x, g, shift, scale):
    y = x * lax.rsqrt(jnp.mean(x * x, axis=-1, keepdims=True) + EPS) * g
    return y * (1.0 + scale) + shift


def _ada_kernel(c_ref, w_ref, b_ref, o_ref):
    a = _silu(c_ref[...]).astype(BF16)
    o_ref[...] = _dot(a, w_ref[...].astype(BF16)) + b_ref[...]


def _ada(c_all, w_ada, b_ada):
    depth, d, nd = w_ada.shape
    nc = c_all.shape[0]
    tn = nd // 6
    assert tn % LANES == 0
    return pl.pallas_call(
        _ada_kernel,
        grid=(depth, nd // tn),
        in_specs=[pl.BlockSpec((nc, d), lambda l, j: (0, 0)),
                  pl.BlockSpec((None, d, tn), lambda l, j: (l, 0, j)),
                  pl.BlockSpec((None, 1, tn), lambda l, j: (l, 0, j))],
        out_specs=pl.BlockSpec((None, nc, tn), lambda l, j: (l, 0, j)),
        out_shape=jax.ShapeDtypeStruct((depth, nc, nd), F32),
        compiler_params=_params(2),
        name="ada_mod",
    )(c_all, w_ada, b_ada.reshape(depth, 1, nd))


Q_BLOCKS = 2
NEAR_BLOCKS = 4


def _bias_kernel(scale, rel_ref, o_ref):
    h = pl.program_id(0)
    shape = (NEAR_BLOCKS * MOBA_BLOCK, Q_BLOCKS * MOBA_BLOCK)
    key = lax.broadcasted_iota(jnp.int32, shape, 0)
    qry = lax.broadcasted_iota(jnp.int32, shape, 1)
    d = qry - key + (NEAR_BLOCKS - Q_BLOCKS) * MOBA_BLOCK
    n = jnp.maximum(d, 0)
    max_exact = N_BUCKETS // 2
    nf = jnp.maximum(n, 1).astype(F32)
    large = max_exact + (jnp.log(nf / max_exact) / math.log(MAX_DISTANCE / max_exact)
                         * (N_BUCKETS - max_exact)).astype(jnp.int32)
    large = jnp.minimum(large, N_BUCKETS - 1)
    bucket = jnp.where(n < max_exact, n, large)
    bias = jnp.zeros(shape, F32)
    for b in range(N_BUCKETS):
        bias = jnp.where(bucket == b, rel_ref[h, b], bias)
    o_ref[...] = jnp.where(d < 0, NEG, bias * scale)


def _bias_tiles(rel_bias, scale):
    shape = (NEAR_BLOCKS * MOBA_BLOCK, Q_BLOCKS * MOBA_BLOCK)
    return pl.pallas_call(
        functools.partial(_bias_kernel, scale),
        grid=(N_HEADS,),
        in_specs=[pl.BlockSpec(memory_space=pltpu.SMEM)],
        out_specs=pl.BlockSpec((None,) + shape, lambda h: (h, 0, 0)),
        out_shape=jax.ShapeDtypeStruct((N_HEADS,) + shape, F32),
        compiler_params=_params(1),
        name="t5_bias_tiles",
    )(rel_bias)


def _mod_spec(m, tm, tiles_per_seq):
    if m.ndim == 3:
        return pl.BlockSpec((None, 1, m.shape[-1]), lambda i: (i // tiles_per_seq, 0, 0))
    return pl.BlockSpec((tm, m.shape[-1]), lambda i: (i, 0))


def _const_spec(a):
    zeros = (0,) * a.ndim
    return pl.BlockSpec(a.shape, lambda i: zeros)


def _ffn_kernel(final_norm, x_ref, sh_ref, sc_ref, gt_ref, g_ref, wg_ref, wu_ref, wd_ref, *rest):
    if final_norm:
        gf_ref, o_ref, h_scr, acc_scr = rest
    else:
        o_ref, h_scr, acc_scr = rest
    x = x_ref[...]
    h_scr[...] = _modulated_norm(x, g_ref[...], sh_ref[...], sc_ref[...]).astype(BF16)
    acc_scr[...] = jnp.zeros_like(acc_scr)

    for c in range(wg_ref.shape[1] // MXU_DIM):
        cols = slice(c * MXU_DIM, (c + 1) * MXU_DIM)
        h = h_scr[...]
        g = _dot(h, wg_ref[:, cols])
        u = _dot(h, wu_ref[:, cols])
        acc_scr[...] += _dot((_silu(g) * u).astype(BF16), wd_ref[cols, :])
    y = x + (HALF * gt_ref[...]) * acc_scr[...]
    if final_norm:
        y = y * lax.rsqrt(jnp.mean(y * y, axis=-1, keepdims=True) + EPS) * gf_ref[...]
    o_ref[...] = y


def _ffn(x, sh, sc, gt, g, wg, wu, wd, tiles_per_seq, g_final=None):
    nt, d = x.shape
    tm = min(TOKEN_TILE, nt)
    args = [x, sh, sc, gt, g, wg, wu, wd]
    in_specs = [pl.BlockSpec((tm, d), lambda i: (i, 0)),
                _mod_spec(sh, tm, tiles_per_seq), _mod_spec(sc, tm, tiles_per_seq),
                _mod_spec(gt, tm, tiles_per_seq),
                _const_spec(g), _const_spec(wg), _const_spec(wu), _const_spec(wd)]
    if g_final is not None:
        args.append(g_final)
        in_specs.append(_const_spec(g_final))
    return pl.pallas_call(
        functools.partial(_ffn_kernel, g_final is not None),
        grid=(nt // tm,),
        in_specs=in_specs,
        out_specs=pl.BlockSpec((tm, d), lambda i: (i, 0)),
        out_shape=jax.ShapeDtypeStruct((nt, d), F32),
        scratch_shapes=[pltpu.VMEM((tm, d), BF16), pltpu.VMEM((tm, d), F32)],
        compiler_params=_params(1),
        name="swiglu_ffn",
    )(*args)


def _mixer_in_kernel(transposed_qkv, cw, aw, x_ref, sh_ref, sc_ref, g_ref, w_ref, *rest):
    if transposed_qkv:
        wqkv_ref, u_ref, q_ref, k_ref, v_ref, sga_ref, sgb_ref = rest
    else:
        u_ref, q_ref, k_ref, v_ref, sga_ref, sgb_ref = rest
    d = x_ref.shape[-1]
    h = _modulated_norm(x_ref[...], g_ref[...], sh_ref[...], sc_ref[...]).astype(BF16)
    a = _dot(h, w_ref[:, 0:cw])
    g = _dot(h, w_ref[:, cw:2 * cw])
    u_ref[...] = a * jax.nn.sigmoid(g)
    o = 2 * cw
    if transposed_qkv:
        q_ref[...] = _dot_nt(wqkv_ref[0:aw, :], h)
        k_ref[...] = _dot_nt(wqkv_ref[aw:2 * aw, :], h)
        v_ref[...] = _dot_nt(wqkv_ref[2 * aw:3 * aw, :], h)
    else:
        q_ref[...] = _dot(h, w_ref[:, o:o + aw])
        k_ref[...] = _dot(h, w_ref[:, o + aw:o + 2 * aw])
        v_ref[...] = _dot(h, w_ref[:, o + 2 * aw:o + 3 * aw])
    o += 3 * aw
    sga_ref[...] = jax.nn.sigmoid(_dot(h, w_ref[:, o:o + d]))
    sgb_ref[...] = jax.nn.sigmoid(_dot(h, w_ref[:, o + d:o + 2 * d]))


def _mixer_in(x, sh, sc, g, w_in, cw, aw, tiles_per_seq, w_qkv_t=None):
    nt, d = x.shape
    tm = min(TOKEN_TILE, nt)
    row = lambda i: (i, 0)
    transposed_qkv = w_qkv_t is not None
    if transposed_qkv:
        nseq = nt // (tiles_per_seq * tm)
        kv_shape = jax.ShapeDtypeStruct((nseq, aw, tiles_per_seq * tm), F32)
        kv_spec = pl.BlockSpec((None, aw, tm), lambda i: (i // tiles_per_seq, 0, i % tiles_per_seq))
    else:
        kv_shape = jax.ShapeDtypeStruct((nt, aw), F32)
        kv_spec = pl.BlockSpec((tm, aw), row)
    out_shape = [jax.ShapeDtypeStruct((nt, cw), F32), kv_shape, kv_shape, kv_shape] \
        + [jax.ShapeDtypeStruct((nt, d), F32)] * 2
    out_specs = [pl.BlockSpec((tm, cw), row), kv_spec, kv_spec, kv_spec] \
        + [pl.BlockSpec((tm, d), row)] * 2
    args = [x, sh, sc, g, w_in] + ([w_qkv_t] if transposed_qkv else [])
    in_specs = [pl.BlockSpec((tm, d), row), _mod_spec(sh, tm, tiles_per_seq),
                _mod_spec(sc, tm, tiles_per_seq), _const_spec(g), _const_spec(w_in)] \
        + ([_const_spec(w_qkv_t)] if transposed_qkv else [])
    return pl.pallas_call(
        functools.partial(_mixer_in_kernel, transposed_qkv, cw, aw),
        grid=(nt // tm,),
        in_specs=in_specs,
        out_specs=out_specs,
        out_shape=out_shape,
        compiler_params=_params(1),
        name="mixer_in_proj",
    )(*args)


def _ln_swish(cv, g, b):
    mu = jnp.mean(cv, axis=-1, keepdims=True)
    xc = cv - mu
    y = xc * lax.rsqrt(jnp.mean(xc * xc, axis=-1, keepdims=True) + EPS) * g + b
    return _silu(y)


HALO = 32
CONV_ROWS = 64


def _conv_prompt_kernel(conv_w, cur_ref, halo_ref, w_ref, b_ref, g_ref, bb_ref, o_ref, ext_scr):
    i = pl.program_id(1)
    tt = cur_ref.shape[0]
    n_ext = HALO + tt
    halo = halo_ref[...]
    ext_scr[0, 0:HALO, :] = jnp.where(i > 0, halo, jnp.zeros_like(halo))
    ext_scr[0, HALO:n_ext, :] = cur_ref[...]
    for s in range(1, SUBLANES):
        ext_scr[s, 0:n_ext - s, :] = ext_scr[0, s:n_ext, :]
    base = HALO - (conv_w - 1)
    for r in range(tt // CONV_ROWS):
        acc = jnp.zeros((CONV_ROWS, cur_ref.shape[1]), F32)
        for j in range(conv_w):
            off = r * CONV_ROWS + base + j
            lo = off - off % SUBLANES
            acc = acc + w_ref[j:j + 1, :] * ext_scr[off % SUBLANES, lo:lo + CONV_ROWS, :]
        y = _ln_swish(acc + b_ref[...], g_ref[...], bb_ref[...])
        o_ref[r * CONV_ROWS:(r + 1) * CONV_ROWS, :] = y.astype(o_ref.dtype)


def _conv_prompt(u, conv_w, conv_b, ln_g, ln_b):
    b, s, c = u.shape
    cw = conv_w.shape[0]
    assert cw - 1 <= HALO
    tt = min(TOKEN_TILE, s)
    hb = tt // HALO
    vec = lambda a: a.reshape(1, c)
    const = lambda bi, i: (0, 0)
    return pl.pallas_call(
        functools.partial(_conv_prompt_kernel, cw),
        grid=(b, s // tt),
        in_specs=[pl.BlockSpec((None, tt, c), lambda bi, i: (bi, i, 0)),
                  pl.BlockSpec((None, HALO, c), lambda bi, i: (bi, jnp.maximum(i * hb - 1, 0), 0)),
                  pl.BlockSpec((cw, c), const), pl.BlockSpec((1, c), const),
                  pl.BlockSpec((1, c), const), pl.BlockSpec((1, c), const)],
        out_specs=pl.BlockSpec((None, tt, c), lambda bi, i: (bi, i, 0)),
        out_shape=jax.ShapeDtypeStruct((b, s, c), BF16),
        scratch_shapes=[pltpu.VMEM((SUBLANES, HALO + tt, c), F32)],
        compiler_params=_params(2),
        name="conv_prompt",
    )(u, u, conv_w, vec(conv_b), vec(ln_g), vec(ln_b))


SEQ_CHUNK = 32


def _conv_sample_kernel(hist, st_ref, u_ref, w_ref, b_ref, g_ref, bb_ref, o_ref, ns_ref):
    t_new = u_ref.shape[0]
    row = lambda r: st_ref[r] if r < hist else u_ref[r - hist]
    for t in range(t_new):
        acc = jnp.zeros(u_ref.shape[1:], F32)
        for j in range(w_ref.shape[0]):
            acc = acc + w_ref[j:j + 1, :] * row(t + j)
        o_ref[t] = _ln_swish(acc + b_ref[...], g_ref[...], bb_ref[...])
    for r in range(hist):
        ns_ref[r] = row(r + t_new)


def _conv_sample(layer, state_t, u_t, conv_w, conv_b, ln_g, ln_b):
    _, hist, n, c = state_t.shape
    t = u_t.shape[0]
    assert conv_w.shape[0] == hist + 1
    sc = min(SEQ_CHUNK, n)
    vec = lambda a: a.reshape(1, c)
    const = lambda i: (0, 0)
    return pl.pallas_call(
        functools.partial(_conv_sample_kernel, hist),
        grid=(n // sc,),
        in_specs=[pl.BlockSpec((None, hist, sc, c), lambda i: (layer, 0, i, 0)),
                  pl.BlockSpec((t, sc, c), lambda i: (0, i, 0)),
                  pl.BlockSpec(conv_w.shape, const), pl.BlockSpec((1, c), const),
                  pl.BlockSpec((1, c), const), pl.BlockSpec((1, c), const)],
        out_specs=[pl.BlockSpec((t, sc, c), lambda i: (0, i, 0)),
                   pl.BlockSpec((hist, sc, c), lambda i: (0, i, 0))],
        out_shape=[jax.ShapeDtypeStruct((t, n, c), F32), jax.ShapeDtypeStruct((hist, n, c), F32)],
        compiler_params=_params(1),
        name="conv_sample",
    )(state_t, u_t, conv_w, vec(conv_b), vec(ln_g), vec(ln_b))


SLOTS = 16
PAD_SLOT = 3 * SLOTS
N_PAD = NEAR_BLOCKS - Q_BLOCKS
FAR_GROUP = 2
NEG_BIG = -(2.0 ** 100)


def _attn_prompt_kernel(rounds, nb, *refs):
    for i in range(nb // Q_BLOCKS):
        pl.when(pl.program_id(2) == i)(functools.partial(_attn_prompt_tile, i, rounds, nb, *refs))


def _attn_prompt_tile(i, rounds, nb, far_ref, qt_ref, kt_ref, vt_ref, bias_ref, o_ref,
                      kx_scr, vx_scr, kmx_scr):
    hp = pl.program_id(1)
    blk = MOBA_BLOCK
    tq = Q_BLOCKS * blk

    def fill_scratch():
        lane = lax.broadcasted_iota(jnp.int32, (blk, LANES), 1)
        jrel = lane % HEAD_DIM
        r2 = lax.broadcasted_iota(jnp.int32, (LANES, LANES), 0)
        c2 = lax.broadcasted_iota(jnp.int32, (LANES, LANES), 1)
        kmcols = jnp.zeros((LANES, LANES), F32)
        for p in range(N_PAD):
            vx_scr[p] = jnp.zeros((LANES, blk), BF16)
            for hh in range(HEADS_PER_GROUP):
                spare = (lane // HEAD_DIM) != hh
                kx_scr[hh, p] = jnp.where(spare & (jrel == PAD_SLOT), 1.0, 0.0).astype(BF16)
        for n in range(nb):
            kt_blk = kt_ref[:, n * blk:(n + 1) * blk]
            vx_scr[n + N_PAD] = vt_ref[:, n * blk:(n + 1) * blk].astype(BF16)
            kmcols = jnp.where(c2 == n, jnp.sum(kt_blk, axis=1, keepdims=True) * (1.0 / blk), kmcols)
            k_blk = kt_blk.T
            ind = jnp.where((jrel < PAD_SLOT) & (jrel % SLOTS == n), 1.0, 0.0)
            for hh in range(HEADS_PER_GROUP):
                spare = (lane // HEAD_DIM) != hh
                kx_scr[hh, n + N_PAD] = jnp.where(spare, ind, k_blk).astype(BF16)
        pick = jnp.where((r2 % HEAD_DIM) == c2, 1.0, 0.0)
        pick = jnp.where((r2 % HEAD_DIM) < SLOTS, pick, 0.0)
        kmx = _dot_nt(pick, kmcols, precision=lax.Precision.HIGHEST)
        kmx_scr[...] = jnp.where((r2 // HEAD_DIM) != (c2 // HEAD_DIM), kmx, 0.0)

    if i == 0:
        fill_scratch()

    qt = qt_ref[...]
    gate_t = jnp.dot(kmx_scr[...], qt, precision=lax.Precision.HIGHEST,
                     preferred_element_type=F32)
    n_i = lax.broadcasted_iota(jnp.int32, (SLOTS, tq), 0)
    q_i = lax.broadcasted_iota(jnp.int32, (SLOTS, tq), 1)
    own = Q_BLOCKS * i + q_i // blk
    first_near = Q_BLOCKS * i - N_PAD
    zeros = jnp.zeros((SLOTS, tq), F32)
    groups = []
    for hh in range(HEADS_PER_GROUP):
        base = (1 - hh) * HEAD_DIM
        g = jnp.where(n_i < own, gate_t[base:base + SLOTS, :], NEG)
        rank = zeros
        for m in range(nb):
            gm = g[m:m + 1, :]
            tie = jnp.where(n_i > m, 1.0, 0.0)
            rank = rank + jnp.where(gm > g, 1.0, jnp.where(gm == g, tie, 0.0))
        sel = jnp.where(n_i < own, jnp.where(rank < rounds, 1.0, 0.0), 0.0) > 0.5
        is_far = n_i < first_near
        head = hp * HEADS_PER_GROUP + hh
        far_hi = jnp.where(is_far, jnp.where(sel, far_ref[0, head], NEG_BIG), 0.0)
        far_lo = jnp.where(is_far, jnp.where(sel, far_ref[1, head], 0.0), 0.0)
        near = jnp.where(is_far, 0.0, jnp.where(n_i == own, 0.0, jnp.where(sel, 0.0, NEG_BIG)))
        pad = jnp.where(n_i == 0, NEG_BIG, 0.0)
        groups.append(jnp.concatenate([far_hi, far_lo, near, pad], axis=0))
    spare_vals = jnp.concatenate(groups[::-1], axis=0)
    chan = lax.broadcasted_iota(jnp.int32, qt.shape, 0)
    qs = qt * (HEAD_DIM ** -0.5 * LOG2E)

    def attend(hh, qx, first, n_blocks, bias):
        ss = []
        for j in range(n_blocks):
            s = _dot(kx_scr[hh, first + j], qx)
            if bias is not None:
                s = s + bias_ref[hh, j * blk:(j + 1) * blk, :]
            ss.append(s)
        m = jnp.max(ss[0], axis=0, keepdims=True)
        for s in ss[1:]:
            m = jnp.maximum(m, jnp.max(s, axis=0, keepdims=True))
        return ss, m

    def weigh(hh, ss, m, first):
        rows = slice(hh * HEAD_DIM, (hh + 1) * HEAD_DIM)
        l, acc = None, None
        for j, s in enumerate(ss):
            p = jnp.exp2(s - m)
            lj = jnp.sum(p, axis=0, keepdims=True)
            aj = _dot(vx_scr[first + j][rows, :], p.astype(BF16))
            l = lj if l is None else l + lj
            acc = aj if acc is None else acc + aj
        return l, acc

    heads = range(HEADS_PER_GROUP)
    qx = [jnp.where((chan // HEAD_DIM) == hh, qs, spare_vals).astype(BF16) for hh in heads]
    first = Q_BLOCKS * i
    scored = [attend(hh, qx[hh], first, NEAR_BLOCKS, True) for hh in heads]
    carry = [(scored[hh][1],) + weigh(hh, scored[hh][0], scored[hh][1], first) for hh in heads]

    def far(gi, carry):
        out = []
        first = N_PAD + gi * FAR_GROUP
        scored = [attend(hh, qx[hh], first, FAR_GROUP, None) for hh in heads]
        for hh in heads:
            m, l, acc = carry[hh]
            ss, m_blk = scored[hh]
            m_new = jnp.maximum(m, m_blk)
            alpha = jnp.exp2(m - m_new)
            l_blk, acc_blk = weigh(hh, ss, m_new, first)
            out.append((m_new, alpha * l + l_blk, alpha * acc + acc_blk))
        return tuple(out)

    for gi in range(i * Q_BLOCKS // FAR_GROUP - N_PAD // FAR_GROUP):
        carry = far(gi, carry)
    out_t = jnp.concatenate([acc / l for _, l, acc in carry], axis=0)
    o_ref[...] = out_t.T.astype(o_ref.dtype)


def _attn_prompt(q_t, k_t, v_t, bias_tiles, far_parts):
    b, aw, s = q_t.shape
    nb = s // MOBA_BLOCK
    ng = aw // LANES
    tq = Q_BLOCKS * MOBA_BLOCK
    assert nb <= SLOTS and nb % Q_BLOCKS == 0 and HEADS_PER_GROUP == 2
    assert Q_BLOCKS % FAR_GROUP == 0 and N_PAD % FAR_GROUP == 0
    rounds = min(MOBA_TOPK, nb)
    grid_spec = pltpu.PrefetchScalarGridSpec(
        num_scalar_prefetch=1,
        grid=(b, ng, s // tq),
        in_specs=[pl.BlockSpec((None, LANES, tq), lambda bi, g, i, far: (bi, g, i)),
                  pl.BlockSpec((None, LANES, s), lambda bi, g, i, far: (bi, g, 0)),
                  pl.BlockSpec((None, LANES, s), lambda bi, g, i, far: (bi, g, 0)),
                  pl.BlockSpec((HEADS_PER_GROUP,) + bias_tiles.shape[1:],
                               lambda bi, g, i, far: (g, 0, 0))],
        out_specs=pl.BlockSpec((None, tq, LANES), lambda bi, g, i, far: (bi, i, g)),
        scratch_shapes=[pltpu.VMEM((HEADS_PER_GROUP, nb + N_PAD, MOBA_BLOCK, LANES), BF16),
                        pltpu.VMEM((nb + N_PAD, LANES, MOBA_BLOCK), BF16),
                        pltpu.VMEM((LANES, LANES), F32)],
    )
    return pl.pallas_call(
        functools.partial(_attn_prompt_kernel, rounds, nb),
        grid_spec=grid_spec,
        out_shape=jax.ShapeDtypeStruct((b, s, aw), BF16),
        compiler_params=_params(3),
        name="moba_prompt",
    )(far_parts, q_t, k_t, v_t, bias_tiles)


def _attn_sample_kernel(n_pages, page, rounds, pt_ref, q_ref, kn_ref, vn_ref, bprev_ref, bown_ref,
                        far_ref, *rest):
    kp_refs = rest[:n_pages]
    vp_refs = rest[n_pages:2 * n_pages]
    o_ref, = rest[2 * n_pages:]
    t, aw = q_ref.shape
    rows = t * N_HEADS
    ppb = MOBA_BLOCK // page
    n_past = n_pages // ppb

    r_i = lax.broadcasted_iota(jnp.int32, (rows, aw), 0)
    c_i = lax.broadcasted_iota(jnp.int32, (rows, aw), 1)
    in_head = (c_i // HEAD_DIM) == (r_i % N_HEADS)
    q = q_ref[...]
    q_rep = jnp.zeros((rows, aw), F32)
    for qi in range(t):
        q_rep = jnp.where(r_i // N_HEADS == qi, q[qi:qi + 1, :], q_rep)
    qs = (jnp.where(in_head, q_rep, 0.0) * (HEAD_DIM ** -0.5)).astype(BF16)

    raw = [jnp.concatenate([_dot(qs, kp_refs[n * ppb + j][...].astype(BF16)) for j in range(ppb)],
                           axis=-1) for n in range(n_past)]
    gsum = [jnp.sum(r, axis=-1, keepdims=True) for r in raw]
    s_blocks = []
    for n in range(n_past):
        rank = jnp.zeros((rows, 1), F32)
        for m in range(n_past):
            if m != n:
                tie = 1.0 if m < n else 0.0
                rank = rank + jnp.where(gsum[m] > gsum[n], 1.0,
                                        jnp.where(gsum[m] == gsum[n], tie, 0.0))
        s_n = raw[n] + (bprev_ref[...] if n == n_past - 1 else far_ref[...])
        s_blocks.append(jnp.where(rank < rounds, s_n, NEG))
    pad = jnp.zeros((LANES - t, aw), F32)
    kn = jnp.concatenate([kn_ref[...], pad], axis=0).astype(BF16)
    vn = jnp.concatenate([vn_ref[...], pad], axis=0).astype(BF16)
    s_own = _dot_nt(qs, kn) + bown_ref[...]

    m = jnp.max(s_own, axis=-1, keepdims=True)
    for s_n in s_blocks:
        m = jnp.maximum(m, jnp.max(s_n, axis=-1, keepdims=True))
    p_own = jnp.exp(s_own - m)
    l = jnp.sum(p_own, axis=-1, keepdims=True)
    out = _dot(p_own.astype(BF16), vn)
    for n, s_n in enumerate(s_blocks):
        p_n = jnp.exp(s_n - m)
        l = l + jnp.sum(p_n, axis=-1, keepdims=True)
        p_n = p_n.astype(BF16)
        for j in range(ppb):
            out = out + _dot_nt(p_n[:, j * page:(j + 1) * page], vp_refs[n * ppb + j][...].astype(BF16))
    out = jnp.where(in_head, out / l, 0.0)
    o_ref[...] = jnp.sum(out.reshape(t, N_HEADS, aw), axis=1)


def _attn_sample(layer, q, k_new, v_new, cache_kt, cache_vt, page_table, bprev, bown, far):
    n, t, aw = q.shape
    n_pages = page_table.shape[1]
    page = cache_kt.shape[3]
    n_past_tokens = n_pages * page
    assert page == LANES and MOBA_BLOCK % page == 0 and n_past_tokens % MOBA_BLOCK == 0
    assert t <= LANES and n_past_tokens // MOBA_BLOCK <= LANES
    n_blocks = n_past_tokens // MOBA_BLOCK + 1
    rounds = min(MOBA_TOPK, n_blocks)
    tok = pl.BlockSpec((None, t, aw), lambda i, pt: (i, 0, 0))
    const2 = lambda a: pl.BlockSpec(a.shape, lambda i, pt: (0, 0))
    page_specs = [pl.BlockSpec((None, None, aw, page), lambda i, pt, p=p: (layer, pt[i, p], 0, 0))
                  for p in range(n_pages)]
    grid_spec = pltpu.PrefetchScalarGridSpec(
        num_scalar_prefetch=1,
        grid=(n,),
        in_specs=[tok, tok, tok, const2(bprev), const2(bown), const2(far)] + page_specs + page_specs,
        out_specs=pl.BlockSpec((None, t, aw), lambda i, pt: (i, 0, 0)),
    )
    return pl.pallas_call(
        functools.partial(_attn_sample_kernel, n_pages, page, rounds),
        grid_spec=grid_spec,
        out_shape=jax.ShapeDtypeStruct((n, t, aw), F32),
        compiler_params=_params(1),
        name="moba_sample",
    )(page_table, q, k_new, v_new, bprev, bown, far, *([cache_kt] * n_pages), *([cache_vt] * n_pages))


def _mixer_out_kernel(x_ref, gt_ref, c_ref, a_ref, sga_ref, sgb_ref, wc_ref, wa_ref, wo_ref, o_ref):
    conv_y = _dot(c_ref[...].astype(BF16), wc_ref[...])
    attn_y = _dot(a_ref[...].astype(BF16), wa_ref[...])
    mixed = (sga_ref[...] * conv_y + sgb_ref[...] * attn_y).astype(BF16)
    o_ref[...] = x_ref[...] + gt_ref[...] * _dot(mixed, wo_ref[...])


def _mixer_out(x, gt, cact, att, sga, sgb, wc, wa, wo, tiles_per_seq):
    nt, d = x.shape
    tm = min(TOKEN_TILE, nt)
    row = lambda i: (i, 0)
    return pl.pallas_call(
        _mixer_out_kernel,
        grid=(nt // tm,),
        in_specs=[pl.BlockSpec((tm, d), row), _mod_spec(gt, tm, tiles_per_seq),
                  pl.BlockSpec((tm, cact.shape[1]), row), pl.BlockSpec((tm, att.shape[1]), row),
                  pl.BlockSpec((tm, d), row), pl.BlockSpec((tm, d), row),
                  _const_spec(wc), _const_spec(wa), _const_spec(wo)],
        out_specs=pl.BlockSpec((tm, d), row),
        out_shape=jax.ShapeDtypeStruct((nt, d), F32),
        compiler_params=_params(1),
        name="mixer_out",
    )(x, gt, cact, att, sga, sgb, wc, wa, wo)


def kernel(x_prompt, x_sample, c_prompt, c_sample, cache_k, cache_v, state_conv, page_table, rel_bias, w_ada, b_ada, g_norm, w_ffn_gate, w_ffn_up, w_ffn_down, w_in, conv_w, conv_b, conv_ln_g, conv_ln_b, w_conv_out, w_attn_out, w_o, g_final):
    b, s, d = x_prompt.shape
    n, t, _ = x_sample.shape
    depth = w_ada.shape[0]
    cw = conv_w.shape[2]
    aw = N_HEADS * HEAD_DIM
    d_ff = w_ffn_gate.shape[-1]

    nc = -(-(b + n) // 8) * 8
    c_all = jnp.concatenate([c_prompt, c_sample, jnp.zeros((nc - b - n, d), F32)], axis=0)
    mod = _ada(c_all, w_ada, b_ada).reshape(depth, nc, N_MOD, d)
    tiles = _bias_tiles(rel_bias, 1.0)
    tiles_log2 = _bias_tiles(rel_bias, LOG2E)
    far_bias = rel_bias[:, N_BUCKETS - 1]
    far_log2 = far_bias * LOG2E
    far_hi = far_log2.astype(BF16).astype(F32)
    far_parts = jnp.stack([far_hi, far_log2 - far_hi])

    rows = t * N_HEADS
    own_key = N_PAD * MOBA_BLOCK
    bprev_s = tiles[:, own_key - MOBA_BLOCK:own_key, :t].transpose(2, 0, 1).reshape(rows, MOBA_BLOCK)
    bown_s = jnp.pad(tiles[:, own_key:own_key + t, :t].transpose(2, 0, 1).reshape(rows, t),
                     ((0, 0), (0, LANES - t)), constant_values=NEG)
    far_s = jnp.broadcast_to(jnp.tile(far_bias, t)[:, None], (rows, MOBA_BLOCK))

    cache_kt = cache_k.transpose(0, 1, 3, 4, 2).reshape(cache_k.shape[:2] + (aw, cache_k.shape[2]))
    cache_vt = cache_v.transpose(0, 1, 3, 4, 2).reshape(cache_v.shape[:2] + (aw, cache_v.shape[2]))
    state_t = state_conv.transpose(0, 2, 1, 3)
    hist = state_t.shape[1]

    assert d_ff % MXU_DIM == 0
    wg, wu, wd = w_ffn_gate.astype(BF16), w_ffn_up.astype(BF16), w_ffn_down.astype(BF16)
    w_in_b = w_in.astype(BF16)
    w_qkv_t = w_in[:, :, 2 * cw:2 * cw + 3 * aw].transpose(0, 2, 1).astype(BF16)
    wc_b = w_conv_out.astype(BF16)
    wa_b = w_attn_out.astype(BF16)
    wo_b = w_o.astype(BF16)
    gf = g_final.reshape(1, d)

    def ffn(x, mm, l, i, tiles_per_seq, final):
        o = 6 * i
        return _ffn(x, mm[o], mm[o + 1], mm[o + 2], g_norm[l, 2 * i:2 * i + 1], wg[l, i], wu[l, i],
                    wd[l, i], tiles_per_seq, gf if final else None)

    tm_p = min(TOKEN_TILE, s)
    assert s % tm_p == 0 and s % MOBA_BLOCK == 0
    tps = s // tm_p
    x = x_prompt.reshape(b * s, d)
    kp, vp, cp = [], [], []
    for l in range(depth):
        mm = [mod[l, :b, i][:, None, :] for i in range(N_MOD)]
        x = ffn(x, mm, l, 0, tps, False)
        u, q_t, k_t, v_t, sga, sgb = _mixer_in(x, mm[3], mm[4], g_norm[l, 1:2], w_in_b[l], cw, aw,
                                               tps, w_qkv_t[l])
        u3 = u.reshape(b, s, cw)
        cact = _conv_prompt(u3, conv_w[l], conv_b[l], conv_ln_g[l], conv_ln_b[l])
        att = _attn_prompt(q_t, k_t, v_t, tiles_log2, far_parts)
        x = _mixer_out(x, mm[5], cact.reshape(b * s, cw), att.reshape(b * s, aw), sga, sgb,
                       wc_b[l], wa_b[l], wo_b[l], tps)
        x = ffn(x, mm, l, 1, tps, l == depth - 1)
        kp.append(k_t)
        vp.append(v_t)
        cp.append(u3[:, s - hist:])
    y_p = x.reshape(b, s, d)
    to_cache = lambda z: jnp.stack(z).reshape(depth, b, N_HEADS, HEAD_DIM, s).transpose(0, 1, 4, 2, 3)
    k_p, v_p, c_p = to_cache(kp), to_cache(vp), jnp.stack(cp)

    x = x_sample.reshape(n * t, d)
    ks, vs, cs = [], [], []
    for l in range(depth):
        mm = [jnp.repeat(mod[l, b:b + n, i], t, axis=0) for i in range(N_MOD)]
        x = ffn(x, mm, l, 0, 1, False)
        u, q, k, v, sga, sgb = _mixer_in(x, mm[3], mm[4], g_norm[l, 1:2], w_in_b[l], cw, aw, 1)
        u_t = u.reshape(n, t, cw).transpose(1, 0, 2)
        cact_t, new_state = _conv_sample(l, state_t, u_t, conv_w[l], conv_b[l], conv_ln_g[l],
                                         conv_ln_b[l])
        att = _attn_sample(l, q.reshape(n, t, aw), k.reshape(n, t, aw), v.reshape(n, t, aw),
                           cache_kt, cache_vt, page_table, bprev_s, bown_s, far_s)
        x = _mixer_out(x, mm[5], cact_t.transpose(1, 0, 2).reshape(n * t, cw), att.reshape(n * t, aw),
                       sga, sgb, wc_b[l], wa_b[l], wo_b[l], 1)
        x = ffn(x, mm, l, 1, 1, l == depth - 1)
        ks.append(k.reshape(n, t, N_HEADS, HEAD_DIM))
        vs.append(v.reshape(n, t, N_HEADS, HEAD_DIM))
        cs.append(new_state)
    y_s = x.reshape(n, t, d)
    return (y_p, y_s, k_p, v_p, c_p, jnp.stack(ks), jnp.stack(vs),
            jnp.stack(cs).transpose(0, 2, 1, 3))
```

```python
import functools
import math

import jax
import jax.numpy as jnp
from jax import lax
from jax.experimental import pallas as pl
from jax.experimental.pallas import tpu as pltpu

N_HEADS = 8
HEAD_DIM = 64
MOBA_BLOCK = 256
MOBA_TOPK = 3
N_BUCKETS = 32
MAX_DISTANCE = 128
N_MOD = 9
HALF = 0.5
EPS = 1e-6
NEG = -1e30
LOG2E = math.log2(math.e)

LANES = 128
SUBLANES = 8
HEADS_PER_GROUP = LANES // HEAD_DIM
MXU_DIM = 256
TOKEN_TILE = 512
VMEM_LIMIT = 56 * 1024 * 1024

F32 = jnp.float32
BF16 = jnp.bfloat16


def _params(n_grid_dims):
    return pltpu.CompilerParams(dimension_semantics=("arbitrary",) * n_grid_dims,
                                vmem_limit_bytes=VMEM_LIMIT)


def _dot(a, b):
    return jnp.dot(a, b, preferred_element_type=F32)


def _dot_nt(a, b, precision=None):
    return lax.dot_general(a, b, (((1,), (1,)), ((), ())), precision=precision,
                           preferred_element_type=F32)


def _silu(x):
    return x * jax.nn.sigmoid(x)


def _modulated_norm(x, g, shift, scale):
    y = x * lax.rsqrt(jnp.mean(x * x, axis=-1, keepdims=True) + EPS) * g
    return y * (1.0 + scale) + shift


def _ada_kernel(c_ref, w_ref, b_ref, o_ref):
    a = _silu(c_ref[...]).astype(BF16)
    o_ref[...] = _dot(a, w_ref[...].astype(BF16)) + b_ref[...]


def _ada(c_all, w_ada, b_ada):
    depth, d, nd = w_ada.shape
    nc = c_all.shape[0]
    assert nd == N_MOD * d and d % LANES == 0
    return pl.pallas_call(
        _ada_kernel,
        grid=(depth, N_MOD),
        in_specs=[pl.BlockSpec((nc, d), lambda l, j: (0, 0)),
                  pl.BlockSpec((None, d, d), lambda l, j: (l, 0, j)),
                  pl.BlockSpec((None, 1, d), lambda l, j: (l, 0, j))],
        out_specs=pl.BlockSpec((None, None, nc, d), lambda l, j: (l, j, 0, 0)),
        out_shape=jax.ShapeDtypeStruct((depth, N_MOD, nc, d), F32),
        compiler_params=_params(2),
        name="ada_mod",
    )(c_all, w_ada, b_ada.reshape(depth, 1, nd))


Q_BLOCKS = 2
NEAR_BLOCKS = 4


def _bias_kernel(scale, rel_ref, o_ref):
    h = pl.program_id(0)
    shape = o_ref.shape
    key = lax.broadcasted_iota(jnp.int32, shape, 0)
    qry = lax.broadcasted_iota(jnp.int32, shape, 1)
    d = qry - key + (NEAR_BLOCKS - Q_BLOCKS) * MOBA_BLOCK
    n = jnp.maximum(d, 0)
    max_exact = N_BUCKETS // 2
    nf = jnp.maximum(n, 1).astype(F32)
    large = max_exact + (jnp.log(nf / max_exact) / math.log(MAX_DISTANCE / max_exact)
                         * (N_BUCKETS - max_exact)).astype(jnp.int32)
    large = jnp.minimum(large, N_BUCKETS - 1)
    bucket = jnp.where(n < max_exact, n, large)
    bias = jnp.zeros(shape, F32)
    for b in range(N_BUCKETS):
        bias = jnp.where(bucket == b, rel_ref[h, b], bias)
    o_ref[...] = jnp.where(d < 0, NEG, bias * scale)


def _bias_tiles(rel_bias, scale, n_query):
    shape = (NEAR_BLOCKS * MOBA_BLOCK, n_query)
    return pl.pallas_call(
        functools.partial(_bias_kernel, scale),
        grid=(N_HEADS,),
        in_specs=[pl.BlockSpec(memory_space=pltpu.SMEM)],
        out_specs=pl.BlockSpec((None,) + shape, lambda h: (h, 0, 0)),
        out_shape=jax.ShapeDtypeStruct((N_HEADS,) + shape, F32),
        compiler_params=_params(1),
        name="t5_bias_tiles",
    )(rel_bias)


def _mod_spec(m, tm, tiles_per_seq):
    if m.ndim == 3:
        return pl.BlockSpec((None, 1, m.shape[-1]), lambda i: (i // tiles_per_seq, 0, 0))
    return pl.BlockSpec((tm, m.shape[-1]), lambda i: (i, 0))


def _const_spec(a):
    zeros = (0,) * a.ndim
    return pl.BlockSpec(a.shape, lambda i: zeros)


def _stacked_spec(a, prefix):
    idx = tuple(prefix) + (0, 0)
    return pl.BlockSpec((None,) * len(prefix) + a.shape[-2:], lambda i: idx)


def _ffn_kernel(final_norm, x_ref, sh_ref, sc_ref, gt_ref, g_ref, wg_ref, wu_ref, wd_ref, *rest):
    if final_norm:
        gf_ref, o_ref, h_scr, acc_scr = rest
    else:
        o_ref, h_scr, acc_scr = rest
    x = x_ref[...]
    h_scr[...] = _modulated_norm(x, g_ref[...], sh_ref[...], sc_ref[...]).astype(BF16)
    acc_scr[...] = jnp.zeros_like(acc_scr)

    for c in range(wg_ref.shape[1] // MXU_DIM):
        cols = slice(c * MXU_DIM, (c + 1) * MXU_DIM)
        h = h_scr[...]
        g = _dot(h, wg_ref[:, cols])
        u = _dot(h, wu_ref[:, cols])
        acc_scr[...] += _dot((_silu(g) * u).astype(BF16), wd_ref[cols, :])
    y = x + (HALF * gt_ref[...]) * acc_scr[...]
    if final_norm:
        y = y * lax.rsqrt(jnp.mean(y * y, axis=-1, keepdims=True) + EPS) * gf_ref[...]
    o_ref[...] = y


def _ffn(x, sh, sc, gt, g, wg, wu, wd, widx, tiles_per_seq, g_final=None):
    nt, d = x.shape
    tm = min(TOKEN_TILE, nt)
    args = [x, sh, sc, gt, g, wg, wu, wd]
    in_specs = [pl.BlockSpec((tm, d), lambda i: (i, 0)),
                _mod_spec(sh, tm, tiles_per_seq), _mod_spec(sc, tm, tiles_per_seq),
                _mod_spec(gt, tm, tiles_per_seq),
                _const_spec(g), _stacked_spec(wg, widx), _stacked_spec(wu, widx),
                _stacked_spec(wd, widx)]
    if g_final is not None:
        args.append(g_final)
        in_specs.append(_const_spec(g_final))
    return pl.pallas_call(
        functools.partial(_ffn_kernel, g_final is not None),
        grid=(nt // tm,),
        in_specs=in_specs,
        out_specs=pl.BlockSpec((tm, d), lambda i: (i, 0)),
        out_shape=jax.ShapeDtypeStruct((nt, d), F32),
        scratch_shapes=[pltpu.VMEM((tm, d), BF16), pltpu.VMEM((tm, d), F32)],
        compiler_params=_params(1),
        name="swiglu_ffn",
    )(*args)


def _mixer_in_kernel(transposed_qkv, cw, aw, x_ref, sh_ref, sc_ref, g_ref, w_ref, *rest):
    if transposed_qkv:
        wqkv_ref, u_ref, q_ref, k_ref, v_ref, sga_ref, sgb_ref = rest
    else:
        u_ref, q_ref, k_ref, v_ref, sga_ref, sgb_ref = rest
    d = x_ref.shape[-1]
    h = _modulated_norm(x_ref[...], g_ref[...], sh_ref[...], sc_ref[...]).astype(BF16)
    a = _dot(h, w_ref[:, 0:cw])
    g = _dot(h, w_ref[:, cw:2 * cw])
    u_ref[...] = a * jax.nn.sigmoid(g)
    o = 2 * cw
    if transposed_qkv:
        q_ref[...] = _dot_nt(wqkv_ref[0:aw, :], h)
        k_ref[...] = _dot_nt(wqkv_ref[aw:2 * aw, :], h)
        v_ref[...] = _dot_nt(wqkv_ref[2 * aw:3 * aw, :], h)
    else:
        q_ref[...] = _dot(h, w_ref[:, o:o + aw])
        k_ref[...] = _dot(h, w_ref[:, o + aw:o + 2 * aw])
        v_ref[...] = _dot(h, w_ref[:, o + 2 * aw:o + 3 * aw])
    o += 3 * aw
    sga_ref[...] = jax.nn.sigmoid(_dot(h, w_ref[:, o:o + d])).astype(sga_ref.dtype)
    sgb_ref[...] = jax.nn.sigmoid(_dot(h, w_ref[:, o + d:o + 2 * d])).astype(sgb_ref.dtype)


def _mixer_in(x, sh, sc, g, w_in, layer, cw, aw, tiles_per_seq, w_qkv_t=None):
    nt, d = x.shape
    tm = min(TOKEN_TILE, nt)
    row = lambda i: (i, 0)
    transposed_qkv = w_qkv_t is not None
    if transposed_qkv:
        nseq = nt // (tiles_per_seq * tm)
        kv_shape = jax.ShapeDtypeStruct((nseq, aw, tiles_per_seq * tm), F32)
        kv_spec = pl.BlockSpec((None, aw, tm), lambda i: (i // tiles_per_seq, 0, i % tiles_per_seq))
    else:
        kv_shape = jax.ShapeDtypeStruct((nt, aw), F32)
        kv_spec = pl.BlockSpec((tm, aw), row)
    out_shape = [jax.ShapeDtypeStruct((nt, cw), F32), kv_shape, kv_shape, kv_shape] \
        + [jax.ShapeDtypeStruct((nt, d), BF16)] * 2
    out_specs = [pl.BlockSpec((tm, cw), row), kv_spec, kv_spec, kv_spec] \
        + [pl.BlockSpec((tm, d), row)] * 2
    args = [x, sh, sc, g, w_in] + ([w_qkv_t] if transposed_qkv else [])
    in_specs = [pl.BlockSpec((tm, d), row), _mod_spec(sh, tm, tiles_per_seq),
                _mod_spec(sc, tm, tiles_per_seq), _const_spec(g), _stacked_spec(w_in, (layer,))] \
        + ([_stacked_spec(w_qkv_t, (layer,))] if transposed_qkv else [])
    return pl.pallas_call(
        functools.partial(_mixer_in_kernel, transposed_qkv, cw, aw),
        grid=(nt // tm,),
        in_specs=in_specs,
        out_specs=out_specs,
        out_shape=out_shape,
        compiler_params=_params(1),
        name="mixer_in_proj",
    )(*args)


def _ln_swish(cv, g, b):
    mu = jnp.mean(cv, axis=-1, keepdims=True)
    xc = cv - mu
    y = xc * lax.rsqrt(jnp.mean(xc * xc, axis=-1, keepdims=True) + EPS) * g + b
    return _silu(y)


HALO = 32
CONV_ROWS = 64


def _conv_prompt_kernel(conv_w, cur_ref, halo_ref, w_ref, b_ref, g_ref, bb_ref, o_ref, ext_scr):
    i = pl.program_id(1)
    tt = cur_ref.shape[0]
    n_ext = HALO + tt
    halo = halo_ref[...]
    ext_scr[0, 0:HALO, :] = jnp.where(i > 0, halo, jnp.zeros_like(halo))
    ext_scr[0, HALO:n_ext, :] = cur_ref[...]
    for s in range(1, SUBLANES):
        ext_scr[s, 0:n_ext - s, :] = ext_scr[0, s:n_ext, :]
    base = HALO - (conv_w - 1)
    for r in range(tt // CONV_ROWS):
        acc = jnp.zeros((CONV_ROWS, cur_ref.shape[1]), F32)
        for j in range(conv_w):
            off = r * CONV_ROWS + base + j
            lo = off - off % SUBLANES
            acc = acc + w_ref[j:j + 1, :] * ext_scr[off % SUBLANES, lo:lo + CONV_ROWS, :]
        y = _ln_swish(acc + b_ref[...], g_ref[...], bb_ref[...])
        o_ref[r * CONV_ROWS:(r + 1) * CONV_ROWS, :] = y.astype(o_ref.dtype)


def _conv_prompt(u, conv_w, conv_b, ln_g, ln_b):
    b, s, c = u.shape
    cw = conv_w.shape[0]
    assert cw - 1 <= HALO
    tt = min(TOKEN_TILE, s)
    hb = tt // HALO
    vec = lambda a: a.reshape(1, c)
    const = lambda bi, i: (0, 0)
    return pl.pallas_call(
        functools.partial(_conv_prompt_kernel, cw),
        grid=(b, s // tt),
        in_specs=[pl.BlockSpec((None, tt, c), lambda bi, i: (bi, i, 0)),
                  pl.BlockSpec((None, HALO, c), lambda bi, i: (bi, jnp.maximum(i * hb - 1, 0), 0)),
                  pl.BlockSpec((cw, c), const), pl.BlockSpec((1, c), const),
                  pl.BlockSpec((1, c), const), pl.BlockSpec((1, c), const)],
        out_specs=pl.BlockSpec((None, tt, c), lambda bi, i: (bi, i, 0)),
        out_shape=jax.ShapeDtypeStruct((b, s, c), BF16),
        scratch_shapes=[pltpu.VMEM((SUBLANES, HALO + tt, c), F32)],
        compiler_params=_params(2),
        name="conv_prompt",
    )(u, u, conv_w, vec(conv_b), vec(ln_g), vec(ln_b))


SEQ_CHUNK = 32


def _conv_sample_kernel(hist, st_ref, u_ref, w_ref, b_ref, g_ref, bb_ref, o_ref, ns_ref):
    t_new = u_ref.shape[0]
    row = lambda r: st_ref[r] if r < hist else u_ref[r - hist]
    for t in range(t_new):
        acc = jnp.zeros(u_ref.shape[1:], F32)
        for j in range(w_ref.shape[0]):
            acc = acc + w_ref[j:j + 1, :] * row(t + j)
        o_ref[t] = _ln_swish(acc + b_ref[...], g_ref[...], bb_ref[...])
    for r in range(hist):
        ns_ref[r] = row(r + t_new)


def _conv_sample(layer, state_t, u_t, conv_w, conv_b, ln_g, ln_b):
    _, hist, n, c = state_t.shape
    t = u_t.shape[0]
    assert conv_w.shape[0] == hist + 1
    sc = min(SEQ_CHUNK, n)
    vec = lambda a: a.reshape(1, c)
    const = lambda i: (0, 0)
    return pl.pallas_call(
        functools.partial(_conv_sample_kernel, hist),
        grid=(n // sc,),
        in_specs=[pl.BlockSpec((None, hist, sc, c), lambda i: (layer, 0, i, 0)),
                  pl.BlockSpec((t, sc, c), lambda i: (0, i, 0)),
                  pl.BlockSpec(conv_w.shape, const), pl.BlockSpec((1, c), const),
                  pl.BlockSpec((1, c), const), pl.BlockSpec((1, c), const)],
        out_specs=[pl.BlockSpec((t, sc, c), lambda i: (0, i, 0)),
                   pl.BlockSpec((hist, sc, c), lambda i: (0, i, 0))],
        out_shape=[jax.ShapeDtypeStruct((t, n, c), F32), jax.ShapeDtypeStruct((hist, n, c), F32)],
        compiler_params=_params(1),
        name="conv_sample",
    )(state_t, u_t, conv_w, vec(conv_b), vec(ln_g), vec(ln_b))


SLOTS = 16
PAD_SLOT = 3 * SLOTS
N_PAD = NEAR_BLOCKS - Q_BLOCKS
FAR_GROUP = 4
NEG_BIG = -(2.0 ** 100)


def _attn_prompt_kernel(rounds, nb, *refs):
    for i in range(nb // Q_BLOCKS):
        pl.when(pl.program_id(2) == i)(functools.partial(_attn_prompt_tile, i, rounds, nb, *refs))


def _attn_prompt_tile(i, rounds, nb, far_ref, qt_ref, kt_ref, vt_ref, bias_ref, o_ref,
                      kx_scr, vx_scr, kmx_scr):
    hp = pl.program_id(1)
    blk = MOBA_BLOCK
    tq = Q_BLOCKS * blk

    def fill_scratch():
        lane = lax.broadcasted_iota(jnp.int32, (blk, LANES), 1)
        jrel = lane % HEAD_DIM
        r2 = lax.broadcasted_iota(jnp.int32, (LANES, LANES), 0)
        c2 = lax.broadcasted_iota(jnp.int32, (LANES, LANES), 1)
        kmcols = jnp.zeros((LANES, LANES), F32)
        for p in range(N_PAD):
            vx_scr[p] = jnp.zeros((LANES, blk), BF16)
            for hh in range(HEADS_PER_GROUP):
                spare = (lane // HEAD_DIM) != hh
                kx_scr[hh, p] = jnp.where(spare & (jrel == PAD_SLOT), 1.0, 0.0).astype(BF16)
        for n in range(nb):
            kt_blk = kt_ref[:, n * blk:(n + 1) * blk]
            vx_scr[n + N_PAD] = vt_ref[:, n * blk:(n + 1) * blk].astype(BF16)
            kmcols = jnp.where(c2 == n, jnp.sum(kt_blk, axis=1, keepdims=True) * (1.0 / blk), kmcols)
            k_blk = kt_blk.T
            ind = jnp.where((jrel < PAD_SLOT) & (jrel % SLOTS == n), 1.0, 0.0)
            for hh in range(HEADS_PER_GROUP):
                spare = (lane // HEAD_DIM) != hh
                kx_scr[hh, n + N_PAD] = jnp.where(spare, ind, k_blk).astype(BF16)
        pick = jnp.where((r2 % HEAD_DIM) == c2, 1.0, 0.0)
        pick = jnp.where((r2 % HEAD_DIM) < SLOTS, pick, 0.0)
        kmx = _dot_nt(pick, kmcols, precision=lax.Precision.HIGHEST)
        kmx_scr[...] = jnp.where((r2 // HEAD_DIM) != (c2 // HEAD_DIM), kmx, 0.0)

    if i == 0:
        fill_scratch()

    qt = qt_ref[...]
    gate_t = jnp.dot(kmx_scr[...], qt, precision=lax.Precision.HIGHEST,
                     preferred_element_type=F32)
    n_i = lax.broadcasted_iota(jnp.int32, (SLOTS, tq), 0)
    q_i = lax.broadcasted_iota(jnp.int32, (SLOTS, tq), 1)
    own = Q_BLOCKS * i + q_i // blk
    first_near = Q_BLOCKS * i - N_PAD
    zeros = jnp.zeros((SLOTS, tq), F32)
    groups = []
    for hh in range(HEADS_PER_GROUP):
        base = (1 - hh) * HEAD_DIM
        g = jnp.where(n_i < own, gate_t[base:base + SLOTS, :], NEG)
        rank = zeros
        for m in range(nb):
            gm = g[m:m + 1, :]
            tie = jnp.where(n_i > m, 1.0, 0.0)
            rank = rank + jnp.where(gm > g, 1.0, jnp.where(gm == g, tie, 0.0))
        sel = jnp.where(n_i < own, jnp.where(rank < rounds, 1.0, 0.0), 0.0) > 0.5
        is_far = n_i < first_near
        head = hp * HEADS_PER_GROUP + hh
        far_hi = jnp.where(is_far, jnp.where(sel, far_ref[0, head], NEG_BIG), 0.0)
        far_lo = jnp.where(is_far, jnp.where(sel, far_ref[1, head], 0.0), 0.0)
        near = jnp.where(is_far, 0.0, jnp.where(n_i == own, 0.0, jnp.where(sel, 0.0, NEG_BIG)))
        pad = jnp.where(n_i == 0, NEG_BIG, 0.0)
        groups.append(jnp.concatenate([far_hi, far_lo, near, pad], axis=0))
    spare_vals = jnp.concatenate(groups[::-1], axis=0)
    chan = lax.broadcasted_iota(jnp.int32, qt.shape, 0)
    qs = qt * (HEAD_DIM ** -0.5 * LOG2E)

    def attend(hh, qx, first, n_blocks, bias):
        ss = []
        for j in range(n_blocks):
            s = _dot(kx_scr[hh, first + j], qx)
            if bias is not None:
                s = s + bias_ref[hh, j * blk:(j + 1) * blk, :]
            ss.append(s)
        m = jnp.max(ss[0], axis=0, keepdims=True)
        for s in ss[1:]:
            m = jnp.maximum(m, jnp.max(s, axis=0, keepdims=True))
        return ss, m

    def weigh(hh, ss, m, first):
        rows = slice(hh * HEAD_DIM, (hh + 1) * HEAD_DIM)
        l, acc = None, None
        for j, s in enumerate(ss):
            p = jnp.exp2(s - m)
            lj = jnp.sum(p, axis=0, keepdims=True)
            aj = _dot(vx_scr[first + j][rows, :], p.astype(BF16))
            l = lj if l is None else l + lj
            acc = aj if acc is None else acc + aj
        return l, acc

    heads = range(HEADS_PER_GROUP)
    qx = [jnp.where((chan // HEAD_DIM) == hh, qs, spare_vals).astype(BF16) for hh in heads]
    first = Q_BLOCKS * i
    scored = [attend(hh, qx[hh], first, NEAR_BLOCKS, True) for hh in heads]
    carry = [(scored[hh][1],) + weigh(hh, scored[hh][0], scored[hh][1], first) for hh in heads]

    def far(first, n_blocks, carry):
        out = []
        scored = [attend(hh, qx[hh], first, n_blocks, None) for hh in heads]
        for hh in heads:
            m, l, acc = carry[hh]
            ss, m_blk = scored[hh]
            m_new = jnp.maximum(m, m_blk)
            alpha = jnp.exp2(m - m_new)
            l_blk, acc_blk = weigh(hh, ss, m_new, first)
            out.append((m_new, alpha * l + l_blk, alpha * acc + acc_blk))
        return tuple(out)

    for start in range(0, max(first_near, 0), FAR_GROUP):
        carry = far(N_PAD + start, min(FAR_GROUP, first_near - start), carry)
    out_t = jnp.concatenate([acc / l for _, l, acc in carry], axis=0)
    o_ref[...] = out_t.T.astype(o_ref.dtype)


def _attn_prompt(q_t, k_t, v_t, bias_tiles, far_parts):
    b, aw, s = q_t.shape
    nb = s // MOBA_BLOCK
    ng = aw // LANES
    tq = Q_BLOCKS * MOBA_BLOCK
    assert nb <= SLOTS and nb % Q_BLOCKS == 0 and HEADS_PER_GROUP == 2
    rounds = min(MOBA_TOPK, nb)
    grid_spec = pltpu.PrefetchScalarGridSpec(
        num_scalar_prefetch=1,
        grid=(b, ng, s // tq),
        in_specs=[pl.BlockSpec((None, LANES, tq), lambda bi, g, i, far: (bi, g, i)),
                  pl.BlockSpec((None, LANES, s), lambda bi, g, i, far: (bi, g, 0)),
                  pl.BlockSpec((None, LANES, s), lambda bi, g, i, far: (bi, g, 0)),
                  pl.BlockSpec((HEADS_PER_GROUP,) + bias_tiles.shape[1:],
                               lambda bi, g, i, far: (g, 0, 0))],
        out_specs=pl.BlockSpec((None, tq, LANES), lambda bi, g, i, far: (bi, i, g)),
        scratch_shapes=[pltpu.VMEM((HEADS_PER_GROUP, nb + N_PAD, MOBA_BLOCK, LANES), BF16),
                        pltpu.VMEM((nb + N_PAD, LANES, MOBA_BLOCK), BF16),
                        pltpu.VMEM((LANES, LANES), F32)],
    )
    return pl.pallas_call(
        functools.partial(_attn_prompt_kernel, rounds, nb),
        grid_spec=grid_spec,
        out_shape=jax.ShapeDtypeStruct((b, s, aw), BF16),
        compiler_params=_params(3),
        name="moba_prompt",
    )(far_parts, q_t, k_t, v_t, bias_tiles)


def _attn_sample_kernel(n_pages, page, rounds, pt_ref, q_ref, kn_ref, vn_ref, bprev_ref, bown_ref,
                        far_ref, *rest):
    kp_refs = rest[:n_pages]
    vp_refs = rest[n_pages:2 * n_pages]
    o_ref, = rest[2 * n_pages:]
    t, aw = q_ref.shape
    rows = t * N_HEADS
    ppb = MOBA_BLOCK // page
    n_past = n_pages // ppb

    r_i = lax.broadcasted_iota(jnp.int32, (rows, aw), 0)
    c_i = lax.broadcasted_iota(jnp.int32, (rows, aw), 1)
    in_head = (c_i // HEAD_DIM) == (r_i % N_HEADS)
    q = q_ref[...]
    q_rep = jnp.zeros((rows, aw), F32)
    for qi in range(t):
        q_rep = jnp.where(r_i // N_HEADS == qi, q[qi:qi + 1, :], q_rep)
    qs = (jnp.where(in_head, q_rep, 0.0) * (HEAD_DIM ** -0.5)).astype(BF16)

    raw = [jnp.concatenate([_dot(qs, kp_refs[n * ppb + j][...].astype(BF16)) for j in range(ppb)],
                           axis=-1) for n in range(n_past)]
    gsum = [jnp.sum(r, axis=-1, keepdims=True) for r in raw]
    s_blocks = []
    for n in range(n_past):
        rank = jnp.zeros((rows, 1), F32)
        for m in range(n_past):
            if m != n:
                tie = 1.0 if m < n else 0.0
                rank = rank + jnp.where(gsum[m] > gsum[n], 1.0,
                                        jnp.where(gsum[m] == gsum[n], tie, 0.0))
        s_n = raw[n] + (bprev_ref[...] if n == n_past - 1 else far_ref[...])
        s_blocks.append(jnp.where(rank < rounds, s_n, NEG))
    pad = jnp.zeros((LANES - t, aw), F32)
    kn = jnp.concatenate([kn_ref[...], pad], axis=0).astype(BF16)
    vn = jnp.concatenate([vn_ref[...], pad], axis=0).astype(BF16)
    s_own = _dot_nt(qs, kn) + bown_ref[...]

    m = jnp.max(s_own, axis=-1, keepdims=True)
    for s_n in s_blocks:
        m = jnp.maximum(m, jnp.max(s_n, axis=-1, keepdims=True))
    p_own = jnp.exp(s_own - m)
    l = jnp.sum(p_own, axis=-1, keepdims=True)
    out = _dot(p_own.astype(BF16), vn)
    for n, s_n in enumerate(s_blocks):
        p_n = jnp.exp(s_n - m)
        l = l + jnp.sum(p_n, axis=-1, keepdims=True)
        p_n = p_n.astype(BF16)
        for j in range(ppb):
            out = out + _dot_nt(p_n[:, j * page:(j + 1) * page], vp_refs[n * ppb + j][...].astype(BF16))
    out = jnp.where(in_head, out / l, 0.0)
    o_ref[...] = jnp.sum(out.reshape(t, N_HEADS, aw), axis=1)


def _attn_sample(layer, q, k_new, v_new, cache_kt, cache_vt, page_table, bprev, bown, far):
    n, t, aw = q.shape
    n_pages = page_table.shape[1]
    page = cache_kt.shape[3]
    n_past_tokens = n_pages * page
    assert page == LANES and MOBA_BLOCK % page == 0 and n_past_tokens % MOBA_BLOCK == 0
    assert t <= LANES and n_past_tokens // MOBA_BLOCK <= LANES
    n_blocks = n_past_tokens // MOBA_BLOCK + 1
    rounds = min(MOBA_TOPK, n_blocks)
    tok = pl.BlockSpec((None, t, aw), lambda i, pt: (i, 0, 0))
    const2 = lambda a: pl.BlockSpec(a.shape, lambda i, pt: (0, 0))
    page_specs = [pl.BlockSpec((None, None, aw, page), lambda i, pt, p=p: (layer, pt[i, p], 0, 0))
                  for p in range(n_pages)]
    grid_spec = pltpu.PrefetchScalarGridSpec(
        num_scalar_prefetch=1,
        grid=(n,),
        in_specs=[tok, tok, tok, const2(bprev), const2(bown), const2(far)] + page_specs + page_specs,
        out_specs=pl.BlockSpec((None, t, aw), lambda i, pt: (i, 0, 0)),
    )
    return pl.pallas_call(
        functools.partial(_attn_sample_kernel, n_pages, page, rounds),
        grid_spec=grid_spec,
        out_shape=jax.ShapeDtypeStruct((n, t, aw), F32),
        compiler_params=_params(1),
        name="moba_sample",
    )(page_table, q, k_new, v_new, bprev, bown, far, *([cache_kt] * n_pages), *([cache_vt] * n_pages))


def _mixer_out_kernel(x_ref, gt_ref, c_ref, a_ref, sga_ref, sgb_ref, wc_ref, wa_ref, wo_ref, o_ref):
    conv_y = _dot(c_ref[...].astype(BF16), wc_ref[...])
    attn_y = _dot(a_ref[...].astype(BF16), wa_ref[...])
    mixed = (sga_ref[...].astype(F32) * conv_y + sgb_ref[...].astype(F32) * attn_y).astype(BF16)
    o_ref[...] = x_ref[...] + gt_ref[...] * _dot(mixed, wo_ref[...])


def _mixer_out(x, gt, cact, att, sga, sgb, wc, wa, wo, layer, tiles_per_seq):
    nt, d = x.shape
    tm = min(TOKEN_TILE, nt)
    row = lambda i: (i, 0)
    return pl.pallas_call(
        _mixer_out_kernel,
        grid=(nt // tm,),
        in_specs=[pl.BlockSpec((tm, d), row), _mod_spec(gt, tm, tiles_per_seq),
                  pl.BlockSpec((tm, cact.shape[1]), row), pl.BlockSpec((tm, att.shape[1]), row),
                  pl.BlockSpec((tm, d), row), pl.BlockSpec((tm, d), row),
                  _stacked_spec(wc, (layer,)), _stacked_spec(wa, (layer,)),
                  _stacked_spec(wo, (layer,))],
        out_specs=pl.BlockSpec((tm, d), row),
        out_shape=jax.ShapeDtypeStruct((nt, d), F32),
        compiler_params=_params(1),
        name="mixer_out",
    )(x, gt, cact, att, sga, sgb, wc, wa, wo)


def kernel(x_prompt, x_sample, c_prompt, c_sample, cache_k, cache_v, state_conv, page_table, rel_bias, w_ada, b_ada, g_norm, w_ffn_gate, w_ffn_up, w_ffn_down, w_in, conv_w, conv_b, conv_ln_g, conv_ln_b, w_conv_out, w_attn_out, w_o, g_final):
    b, s, d = x_prompt.shape
    n, t, _ = x_sample.shape
    depth = w_ada.shape[0]
    cw = conv_w.shape[2]
    aw = N_HEADS * HEAD_DIM
    d_ff = w_ffn_gate.shape[-1]

    nc = -(-(b + n) // 8) * 8
    c_all = jnp.concatenate([c_prompt, c_sample, jnp.zeros((nc - b - n, d), F32)], axis=0)
    mod = _ada(c_all, w_ada, b_ada)
    assert t <= LANES
    tiles = _bias_tiles(rel_bias, 1.0, LANES)
    tiles_log2 = _bias_tiles(rel_bias, LOG2E, Q_BLOCKS * MOBA_BLOCK)
    far_bias = rel_bias[:, N_BUCKETS - 1]
    far_log2 = far_bias * LOG2E
    far_hi = far_log2.astype(BF16).astype(F32)
    far_parts = jnp.stack([far_hi, far_log2 - far_hi])

    rows = t * N_HEADS
    own_key = N_PAD * MOBA_BLOCK
    bprev_s = tiles[:, own_key - MOBA_BLOCK:own_key, :t].transpose(2, 0, 1).reshape(rows, MOBA_BLOCK)
    bown_s = jnp.pad(tiles[:, own_key:own_key + t, :t].transpose(2, 0, 1).reshape(rows, t),
                     ((0, 0), (0, LANES - t)), constant_values=NEG)
    far_s = jnp.broadcast_to(jnp.tile(far_bias, t)[:, None], (rows, MOBA_BLOCK))

    cache_kt = cache_k.transpose(0, 1, 3, 4, 2).reshape(cache_k.shape[:2] + (aw, cache_k.shape[2]))
    cache_vt = cache_v.transpose(0, 1, 3, 4, 2).reshape(cache_v.shape[:2] + (aw, cache_v.shape[2]))
    state_t = state_conv.transpose(0, 2, 1, 3)
    hist = state_t.shape[1]

    assert d_ff % MXU_DIM == 0
    wg, wu, wd = w_ffn_gate.astype(BF16), w_ffn_up.astype(BF16), w_ffn_down.astype(BF16)
    w_in_b = w_in.astype(BF16)
    w_qkv_t = w_in[:, :, 2 * cw:2 * cw + 3 * aw].transpose(0, 2, 1).astype(BF16)
    wc_b = w_conv_out.astype(BF16)
    wa_b = w_attn_out.astype(BF16)
    wo_b = w_o.astype(BF16)
    gf = g_final.reshape(1, d)

    def ffn(x, mm, l, i, tiles_per_seq, final):
        o = 6 * i
        return _ffn(x, mm[o], mm[o + 1], mm[o + 2], g_norm[l, 2 * i:2 * i + 1], wg, wu, wd, (l, i),
                    tiles_per_seq, gf if final else None)

    tm_p = min(TOKEN_TILE, s)
    assert s % tm_p == 0 and s % MOBA_BLOCK == 0
    tps = s // tm_p
    x = x_prompt.reshape(b * s, d)
    kp, vp, cp = [], [], []
    for l in range(depth):
        mm = [mod[l, i, :b][:, None, :] for i in range(N_MOD)]
        x = ffn(x, mm, l, 0, tps, False)
        u, q_t, k_t, v_t, sga, sgb = _mixer_in(x, mm[3], mm[4], g_norm[l, 1:2], w_in_b, l, cw, aw,
                                               tps, w_qkv_t)
        u3 = u.reshape(b, s, cw)
        cact = _conv_prompt(u3, conv_w[l], conv_b[l], conv_ln_g[l], conv_ln_b[l])
        att = _attn_prompt(q_t, k_t, v_t, tiles_log2, far_parts)
        x = _mixer_out(x, mm[5], cact.reshape(b * s, cw), att.reshape(b * s, aw), sga, sgb,
                       wc_b, wa_b, wo_b, l, tps)
        x = ffn(x, mm, l, 1, tps, l == depth - 1)
        kp.append(k_t)
        vp.append(v_t)
        cp.append(u3[:, s - hist:])
    y_p = x.reshape(b, s, d)
    to_cache = lambda z: jnp.stack(z).reshape(depth, b, N_HEADS, HEAD_DIM, s).transpose(0, 1, 4, 2, 3)
    k_p, v_p, c_p = to_cache(kp), to_cache(vp), jnp.stack(cp)

    x = x_sample.reshape(n * t, d)
    ks, vs, cs = [], [], []
    for l in range(depth):
        mm = [jnp.repeat(mod[l, i, b:b + n], t, axis=0) for i in range(N_MOD)]
        x = ffn(x, mm, l, 0, 1, False)
        u, q, k, v, sga, sgb = _mixer_in(x, mm[3], mm[4], g_norm[l, 1:2], w_in_b, l, cw, aw, 1)
        u_t = u.reshape(n, t, cw).transpose(1, 0, 2)
        cact_t, new_state = _conv_sample(l, state_t, u_t, conv_w[l], conv_b[l], conv_ln_g[l],
                                         conv_ln_b[l])
        att = _attn_sample(l, q.reshape(n, t, aw), k.reshape(n, t, aw), v.reshape(n, t, aw),
                           cache_kt, cache_vt, page_table, bprev_s, bown_s, far_s)
        x = _mixer_out(x, mm[5], cact_t.transpose(1, 0, 2).reshape(n * t, cw), att.reshape(n * t, aw),
                       sga, sgb, wc_b, wa_b, wo_b, l, 1)
        x = ffn(x, mm, l, 1, 1, l == depth - 1)
        ks.append(k.reshape(n, t, N_HEADS, HEAD_DIM))
        vs.append(v.reshape(n, t, N_HEADS, HEAD_DIM))
        cs.append(new_state)
    y_s = x.reshape(n, t, d)
    return (y_p, y_s, k_p, v_p, c_p, jnp.stack(ks), jnp.stack(vs),
            jnp.stack(cs).transpose(0, 2, 1, 3))
```

```python
import functools
import math

import jax
import jax.numpy as jnp
from jax import lax
from jax.experimental import pallas as pl
from jax.experimental.pallas import tpu as pltpu

N_HEADS = 8
HEAD_DIM = 64
MOBA_BLOCK = 256
MOBA_TOPK = 3
N_BUCKETS = 32
MAX_DISTANCE = 128
N_MOD = 9
HALF = 0.5
EPS = 1e-6
NEG = -1e30
LOG2E = math.log2(math.e)

LANES = 128
SUBLANES = 8
HEADS_PER_GROUP = LANES // HEAD_DIM
MXU_DIM = 256
TOKEN_TILE = 512
VMEM_LIMIT = 56 * 1024 * 1024

F32 = jnp.float32
BF16 = jnp.bfloat16


def _params(n_grid_dims):
    return pltpu.CompilerParams(dimension_semantics=("arbitrary",) * n_grid_dims,
                                vmem_limit_bytes=VMEM_LIMIT)


def _dot(a, b):
    return jnp.dot(a, b, preferred_element_type=F32)


def _dot_nt(a, b, precision=None):
    return lax.dot_general(a, b, (((1,), (1,)), ((), ())), precision=precision,
                           preferred_element_type=F32)


def _silu(x):
    return x * jax.nn.sigmoid(x)


def _modulated_norm(x, g, shift, scale):
    y = x * lax.rsqrt(jnp.mean(x * x, axis=-1, keepdims=True) + EPS) * g
    return y * (1.0 + scale) + shift


def _ada_kernel(c_ref, w_ref, b_ref, o_ref):
    a = _silu(c_ref[...]).astype(BF16)
    o_ref[...] = _dot(a, w_ref[...].astype(BF16)) + b_ref[...]


def _ada(c_all, w_ada, b_ada):
    depth, d, nd = w_ada.shape
    nc = c_all.shape[0]
    assert nd == N_MOD * d and d % LANES == 0
    return pl.pallas_call(
        _ada_kernel,
        grid=(depth, N_MOD),
        in_specs=[pl.BlockSpec((nc, d), lambda l, j: (0, 0)),
                  pl.BlockSpec((None, d, d), lambda l, j: (l, 0, j)),
                  pl.BlockSpec((None, 1, d), lambda l, j: (l, 0, j))],
        out_specs=pl.BlockSpec((None, None, nc, d), lambda l, j: (l, j, 0, 0)),
        out_shape=jax.ShapeDtypeStruct((depth, N_MOD, nc, d), F32),
        compiler_params=_params(2),
        name="ada_mod",
    )(c_all, w_ada, b_ada.reshape(depth, 1, nd))


Q_BLOCKS = 2
NEAR_BLOCKS = 4


def _bias_kernel(scale, rel_ref, o_ref):
    h = pl.program_id(0)
    shape = o_ref.shape
    key = lax.broadcasted_iota(jnp.int32, shape, 0)
    qry = lax.broadcasted_iota(jnp.int32, shape, 1)
    d = qry - key + (NEAR_BLOCKS - Q_BLOCKS) * MOBA_BLOCK
    n = jnp.maximum(d, 0)
    max_exact = N_BUCKETS // 2
    nf = jnp.maximum(n, 1).astype(F32)
    large = max_exact + (jnp.log(nf / max_exact) / math.log(MAX_DISTANCE / max_exact)
                         * (N_BUCKETS - max_exact)).astype(jnp.int32)
    large = jnp.minimum(large, N_BUCKETS - 1)
    bucket = jnp.where(n < max_exact, n, large)
    bias = jnp.zeros(shape, F32)
    for b in range(N_BUCKETS):
        bias = jnp.where(bucket == b, rel_ref[h, b], bias)
    o_ref[...] = jnp.where(d < 0, NEG, bias * scale)


def _bias_tiles(rel_bias, scale, n_query):
    shape = (NEAR_BLOCKS * MOBA_BLOCK, n_query)
    return pl.pallas_call(
        functools.partial(_bias_kernel, scale),
        grid=(N_HEADS,),
        in_specs=[pl.BlockSpec(memory_space=pltpu.SMEM)],
        out_specs=pl.BlockSpec((None,) + shape, lambda h: (h, 0, 0)),
        out_shape=jax.ShapeDtypeStruct((N_HEADS,) + shape, F32),
        compiler_params=_params(1),
        name="t5_bias_tiles",
    )(rel_bias)


def _mod_spec(m, tm, tiles_per_seq):
    if m.ndim == 3:
        return pl.BlockSpec((None, 1, m.shape[-1]), lambda i: (i // tiles_per_seq, 0, 0))
    return pl.BlockSpec((tm, m.shape[-1]), lambda i: (i, 0))


def _const_spec(a):
    zeros = (0,) * a.ndim
    return pl.BlockSpec(a.shape, lambda i: zeros)


def _stacked_spec(a, prefix):
    idx = tuple(prefix) + (0, 0)
    return pl.BlockSpec((None,) * len(prefix) + a.shape[-2:], lambda i: idx)


def _ffn_kernel(final_norm, x_ref, sh_ref, sc_ref, gt_ref, g_ref, wg_ref, wu_ref, wd_ref, *rest):
    if final_norm:
        gf_ref, o_ref, h_scr, acc_scr = rest
    else:
        o_ref, h_scr, acc_scr = rest
    x = x_ref[...]
    h_scr[...] = _modulated_norm(x, g_ref[...], sh_ref[...], sc_ref[...]).astype(BF16)
    acc_scr[...] = jnp.zeros_like(acc_scr)

    for c in range(wg_ref.shape[1] // MXU_DIM):
        cols = slice(c * MXU_DIM, (c + 1) * MXU_DIM)
        h = h_scr[...]
        g = _dot(h, wg_ref[:, cols])
        u = _dot(h, wu_ref[:, cols])
        acc_scr[...] += _dot((_silu(g) * u).astype(BF16), wd_ref[cols, :])
    y = x + (HALF * gt_ref[...]) * acc_scr[...]
    if final_norm:
        y = y * lax.rsqrt(jnp.mean(y * y, axis=-1, keepdims=True) + EPS) * gf_ref[...]
    o_ref[...] = y


def _ffn(x, sh, sc, gt, g, wg, wu, wd, widx, tiles_per_seq, g_final=None):
    nt, d = x.shape
    tm = min(TOKEN_TILE, nt)
    args = [x, sh, sc, gt, g, wg, wu, wd]
    in_specs = [pl.BlockSpec((tm, d), lambda i: (i, 0)),
                _mod_spec(sh, tm, tiles_per_seq), _mod_spec(sc, tm, tiles_per_seq),
                _mod_spec(gt, tm, tiles_per_seq),
                _const_spec(g), _stacked_spec(wg, widx), _stacked_spec(wu, widx),
                _stacked_spec(wd, widx)]
    if g_final is not None:
        args.append(g_final)
        in_specs.append(_const_spec(g_final))
    return pl.pallas_call(
        functools.partial(_ffn_kernel, g_final is not None),
        grid=(nt // tm,),
        in_specs=in_specs,
        out_specs=pl.BlockSpec((tm, d), lambda i: (i, 0)),
        out_shape=jax.ShapeDtypeStruct((nt, d), F32),
        scratch_shapes=[pltpu.VMEM((tm, d), BF16), pltpu.VMEM((tm, d), F32)],
        compiler_params=_params(1),
        name="swiglu_ffn",
    )(*args)


def _mixer_in_kernel(transposed_qkv, cw, aw, x_ref, sh_ref, sc_ref, g_ref, w_ref, *rest):
    if transposed_qkv:
        wqkv_ref, _, _, u_ref, q_ref, k_ref, v_ref, sga_ref, sgb_ref = rest
    else:
        u_ref, q_ref, k_ref, v_ref, sga_ref, sgb_ref = rest
    d = x_ref.shape[-1]
    h = _modulated_norm(x_ref[...], g_ref[...], sh_ref[...], sc_ref[...]).astype(BF16)
    a = _dot(h, w_ref[:, 0:cw])
    g = _dot(h, w_ref[:, cw:2 * cw])
    u_ref[...] = a * jax.nn.sigmoid(g)
    o = 2 * cw
    if transposed_qkv:
        q_ref[...] = _dot_nt(wqkv_ref[0:aw, :], h)
        k_ref[...] = _dot_nt(wqkv_ref[aw:2 * aw, :], h)
        v_ref[...] = _dot_nt(wqkv_ref[2 * aw:3 * aw, :], h)
    else:
        q_ref[...] = _dot(h, w_ref[:, o:o + aw])
        k_ref[...] = _dot(h, w_ref[:, o + aw:o + 2 * aw])
        v_ref[...] = _dot(h, w_ref[:, o + 2 * aw:o + 3 * aw])
    o += 3 * aw
    sga_ref[...] = jax.nn.sigmoid(_dot(h, w_ref[:, o:o + d])).astype(sga_ref.dtype)
    sgb_ref[...] = jax.nn.sigmoid(_dot(h, w_ref[:, o + d:o + 2 * d])).astype(sgb_ref.dtype)


def _mixer_in(x, sh, sc, g, w_in, layer, cw, aw, tiles_per_seq, w_qkv_t=None, kv_stack=None):
    nt, d = x.shape
    tm = min(TOKEN_TILE, nt)
    row = lambda i: (i, 0)
    transposed_qkv = w_qkv_t is not None
    args = [x, sh, sc, g, w_in]
    in_specs = [pl.BlockSpec((tm, d), row), _mod_spec(sh, tm, tiles_per_seq),
                _mod_spec(sc, tm, tiles_per_seq), _const_spec(g), _stacked_spec(w_in, (layer,))]
    aliases = {}
    if transposed_qkv:
        nseq = nt // (tiles_per_seq * tm)
        q_shape = jax.ShapeDtypeStruct((nseq, aw, tiles_per_seq * tm), F32)
        q_spec = pl.BlockSpec((None, aw, tm), lambda i: (i // tiles_per_seq, 0, i % tiles_per_seq))
        kv_shape = jax.ShapeDtypeStruct(kv_stack[0].shape, F32)
        kv_spec = pl.BlockSpec((None, None, aw, tm),
                               lambda i: (layer, i // tiles_per_seq, 0, i % tiles_per_seq))
        args += [w_qkv_t, kv_stack[0], kv_stack[1]]
        in_specs += [_stacked_spec(w_qkv_t, (layer,)), pl.BlockSpec(memory_space=pl.ANY),
                     pl.BlockSpec(memory_space=pl.ANY)]
        aliases = {len(args) - 2: 2, len(args) - 1: 3}
    else:
        q_shape = kv_shape = jax.ShapeDtypeStruct((nt, aw), F32)
        q_spec = kv_spec = pl.BlockSpec((tm, aw), row)
    out_shape = [jax.ShapeDtypeStruct((nt, cw), F32), q_shape, kv_shape, kv_shape] \
        + [jax.ShapeDtypeStruct((nt, d), BF16)] * 2
    out_specs = [pl.BlockSpec((tm, cw), row), q_spec, kv_spec, kv_spec] \
        + [pl.BlockSpec((tm, d), row)] * 2
    return pl.pallas_call(
        functools.partial(_mixer_in_kernel, transposed_qkv, cw, aw),
        grid=(nt // tm,),
        in_specs=in_specs,
        out_specs=out_specs,
        out_shape=out_shape,
        input_output_aliases=aliases,
        compiler_params=_params(1),
        name="mixer_in_proj",
    )(*args)


def _ln_swish(cv, g, b):
    mu = jnp.mean(cv, axis=-1, keepdims=True)
    xc = cv - mu
    y = xc * lax.rsqrt(jnp.mean(xc * xc, axis=-1, keepdims=True) + EPS) * g + b
    return _silu(y)


HALO = 32
CONV_ROWS = 64


def _conv_prompt_kernel(conv_w, cur_ref, halo_ref, w_ref, b_ref, g_ref, bb_ref, o_ref, ext_scr):
    i = pl.program_id(1)
    tt = cur_ref.shape[0]
    n_ext = HALO + tt
    halo = halo_ref[...]
    ext_scr[0, 0:HALO, :] = jnp.where(i > 0, halo, jnp.zeros_like(halo))
    ext_scr[0, HALO:n_ext, :] = cur_ref[...]
    for s in range(1, SUBLANES):
        ext_scr[s, 0:n_ext - s, :] = ext_scr[0, s:n_ext, :]
    base = HALO - (conv_w - 1)
    for r in range(tt // CONV_ROWS):
        acc = jnp.zeros((CONV_ROWS, cur_ref.shape[1]), F32)
        for j in range(conv_w):
            off = r * CONV_ROWS + base + j
            lo = off - off % SUBLANES
            acc = acc + w_ref[j:j + 1, :] * ext_scr[off % SUBLANES, lo:lo + CONV_ROWS, :]
        y = _ln_swish(acc + b_ref[...], g_ref[...], bb_ref[...])
        o_ref[r * CONV_ROWS:(r + 1) * CONV_ROWS, :] = y.astype(o_ref.dtype)


def _conv_prompt(u, conv_w, conv_b, ln_g, ln_b):
    b, s, c = u.shape
    cw = conv_w.shape[0]
    assert cw - 1 <= HALO
    tt = min(TOKEN_TILE, s)
    hb = tt // HALO
    vec = lambda a: a.reshape(1, c)
    const = lambda bi, i: (0, 0)
    return pl.pallas_call(
        functools.partial(_conv_prompt_kernel, cw),
        grid=(b, s // tt),
        in_specs=[pl.BlockSpec((None, tt, c), lambda bi, i: (bi, i, 0)),
                  pl.BlockSpec((None, HALO, c), lambda bi, i: (bi, jnp.maximum(i * hb - 1, 0), 0)),
                  pl.BlockSpec((cw, c), const), pl.BlockSpec((1, c), const),
                  pl.BlockSpec((1, c), const), pl.BlockSpec((1, c), const)],
        out_specs=pl.BlockSpec((None, tt, c), lambda bi, i: (bi, i, 0)),
        out_shape=jax.ShapeDtypeStruct((b, s, c), BF16),
        scratch_shapes=[pltpu.VMEM((SUBLANES, HALO + tt, c), F32)],
        compiler_params=_params(2),
        name="conv_prompt",
    )(u, u, conv_w, vec(conv_b), vec(ln_g), vec(ln_b))


SEQ_CHUNK = 32


def _conv_sample_kernel(hist, st_ref, u_ref, w_ref, b_ref, g_ref, bb_ref, o_ref, ns_ref):
    t_new = u_ref.shape[0]
    row = lambda r: st_ref[r] if r < hist else u_ref[r - hist]
    for t in range(t_new):
        acc = jnp.zeros(u_ref.shape[1:], F32)
        for j in range(w_ref.shape[0]):
            acc = acc + w_ref[j:j + 1, :] * row(t + j)
        o_ref[t] = _ln_swish(acc + b_ref[...], g_ref[...], bb_ref[...])
    for r in range(hist):
        ns_ref[r] = row(r + t_new)


def _conv_sample(layer, state_t, u_t, conv_w, conv_b, ln_g, ln_b):
    _, hist, n, c = state_t.shape
    t = u_t.shape[0]
    assert conv_w.shape[0] == hist + 1
    sc = min(SEQ_CHUNK, n)
    vec = lambda a: a.reshape(1, c)
    const = lambda i: (0, 0)
    return pl.pallas_call(
        functools.partial(_conv_sample_kernel, hist),
        grid=(n // sc,),
        in_specs=[pl.BlockSpec((None, hist, sc, c), lambda i: (layer, 0, i, 0)),
                  pl.BlockSpec((t, sc, c), lambda i: (0, i, 0)),
                  pl.BlockSpec(conv_w.shape, const), pl.BlockSpec((1, c), const),
                  pl.BlockSpec((1, c), const), pl.BlockSpec((1, c), const)],
        out_specs=[pl.BlockSpec((t, sc, c), lambda i: (0, i, 0)),
                   pl.BlockSpec((hist, sc, c), lambda i: (0, i, 0))],
        out_shape=[jax.ShapeDtypeStruct((t, n, c), F32), jax.ShapeDtypeStruct((hist, n, c), F32)],
        compiler_params=_params(1),
        name="conv_sample",
    )(state_t, u_t, conv_w, vec(conv_b), vec(ln_g), vec(ln_b))


SLOTS = 16
PAD_SLOT = 3 * SLOTS
N_PAD = NEAR_BLOCKS - Q_BLOCKS
FAR_GROUP = 4
NEG_BIG = -(2.0 ** 100)


def _attn_prompt_kernel(rounds, nb, *refs):
    for i in range(nb // Q_BLOCKS):
        pl.when(pl.program_id(2) == i)(functools.partial(_attn_prompt_tile, i, rounds, nb, *refs))


def _attn_prompt_tile(i, rounds, nb, far_ref, qt_ref, kt_ref, vt_ref, bias_ref, o_ref,
                      kx_scr, vx_scr, kmx_scr):
    hp = pl.program_id(1)
    blk = MOBA_BLOCK
    tq = Q_BLOCKS * blk

    def fill_scratch():
        lane = lax.broadcasted_iota(jnp.int32, (blk, LANES), 1)
        jrel = lane % HEAD_DIM
        r2 = lax.broadcasted_iota(jnp.int32, (LANES, LANES), 0)
        c2 = lax.broadcasted_iota(jnp.int32, (LANES, LANES), 1)
        kmcols = jnp.zeros((LANES, LANES), F32)
        for p in range(N_PAD):
            vx_scr[p] = jnp.zeros((LANES, blk), BF16)
            for hh in range(HEADS_PER_GROUP):
                spare = (lane // HEAD_DIM) != hh
                kx_scr[hh, p] = jnp.where(spare & (jrel == PAD_SLOT), 1.0, 0.0).astype(BF16)
        for n in range(nb):
            kt_blk = kt_ref[:, n * blk:(n + 1) * blk]
            vx_scr[n + N_PAD] = vt_ref[:, n * blk:(n + 1) * blk].astype(BF16)
            kmcols = jnp.where(c2 == n, jnp.sum(kt_blk, axis=1, keepdims=True) * (1.0 / blk), kmcols)
            k_blk = kt_blk.T
            ind = jnp.where((jrel < PAD_SLOT) & (jrel % SLOTS == n), 1.0, 0.0)
            for hh in range(HEADS_PER_GROUP):
                spare = (lane // HEAD_DIM) != hh
                kx_scr[hh, n + N_PAD] = jnp.where(spare, ind, k_blk).astype(BF16)
        pick = jnp.where((r2 % HEAD_DIM) == c2, 1.0, 0.0)
        pick = jnp.where((r2 % HEAD_DIM) < SLOTS, pick, 0.0)
        kmx = _dot_nt(pick, kmcols, precision=lax.Precision.HIGHEST)
        kmx_scr[...] = jnp.where((r2 // HEAD_DIM) != (c2 // HEAD_DIM), kmx, 0.0)

    if i == 0:
        fill_scratch()

    qt = qt_ref[...]
    gate_t = jnp.dot(kmx_scr[...], qt, precision=lax.Precision.HIGHEST,
                     preferred_element_type=F32)
    n_i = lax.broadcasted_iota(jnp.int32, (SLOTS, tq), 0)
    q_i = lax.broadcasted_iota(jnp.int32, (SLOTS, tq), 1)
    own = Q_BLOCKS * i + q_i // blk
    first_near = Q_BLOCKS * i - N_PAD
    zeros = jnp.zeros((SLOTS, tq), F32)
    groups = []
    for hh in range(HEADS_PER_GROUP):
        base = (1 - hh) * HEAD_DIM
        g = jnp.where(n_i < own, gate_t[base:base + SLOTS, :], NEG)
        rank = zeros
        for m in range(nb):
            gm = g[m:m + 1, :]
            tie = jnp.where(n_i > m, 1.0, 0.0)
            rank = rank + jnp.where(gm > g, 1.0, jnp.where(gm == g, tie, 0.0))
        sel = jnp.where(n_i < own, jnp.where(rank < rounds, 1.0, 0.0), 0.0) > 0.5
        is_far = n_i < first_near
        head = hp * HEADS_PER_GROUP + hh
        far_hi = jnp.where(is_far, jnp.where(sel, far_ref[0, head], NEG_BIG), 0.0)
        far_lo = jnp.where(is_far, jnp.where(sel, far_ref[1, head], 0.0), 0.0)
        near = jnp.where(is_far, 0.0, jnp.where(n_i == own, 0.0, jnp.where(sel, 0.0, NEG_BIG)))
        pad = jnp.where(n_i == 0, NEG_BIG, 0.0)
        groups.append(jnp.concatenate([far_hi, far_lo, near, pad], axis=0))
    spare_vals = jnp.concatenate(groups[::-1], axis=0)
    chan = lax.broadcasted_iota(jnp.int32, qt.shape, 0)
    qs = qt * (HEAD_DIM ** -0.5 * LOG2E)

    def attend(hh, qx, first, n_blocks, bias):
        ss = []
        for j in range(n_blocks):
            s = _dot(kx_scr[hh, first + j], qx)
            if bias is not None:
                s = s + bias_ref[hh, j * blk:(j + 1) * blk, :]
            ss.append(s)
        m = jnp.max(ss[0], axis=0, keepdims=True)
        for s in ss[1:]:
            m = jnp.maximum(m, jnp.max(s, axis=0, keepdims=True))
        return ss, m

    def weigh(hh, ss, m, first):
        rows = slice(hh * HEAD_DIM, (hh + 1) * HEAD_DIM)
        l, acc = None, None
        for j, s in enumerate(ss):
            p = jnp.exp2(s - m)
            lj = jnp.sum(p, axis=0, keepdims=True)
            aj = _dot(vx_scr[first + j][rows, :], p.astype(BF16))
            l = lj if l is None else l + lj
            acc = aj if acc is None else acc + aj
        return l, acc

    heads = range(HEADS_PER_GROUP)
    qx = [jnp.where((chan // HEAD_DIM) == hh, qs, spare_vals).astype(BF16) for hh in heads]
    first = Q_BLOCKS * i
    scored = [attend(hh, qx[hh], first, NEAR_BLOCKS, True) for hh in heads]
    carry = [(scored[hh][1],) + weigh(hh, scored[hh][0], scored[hh][1], first) for hh in heads]

    def far(first, n_blocks, carry):
        out = []
        scored = [attend(hh, qx[hh], first, n_blocks, None) for hh in heads]
        for hh in heads:
            m, l, acc = carry[hh]
            ss, m_blk = scored[hh]
            m_new = jnp.maximum(m, m_blk)
            alpha = jnp.exp2(m - m_new)
            l_blk, acc_blk = weigh(hh, ss, m_new, first)
            out.append((m_new, alpha * l + l_blk, alpha * acc + acc_blk))
        return tuple(out)

    for start in range(0, max(first_near, 0), FAR_GROUP):
        carry = far(N_PAD + start, min(FAR_GROUP, first_near - start), carry)
    out_t = jnp.concatenate([acc / l for _, l, acc in carry], axis=0)
    o_ref[...] = out_t.T.astype(o_ref.dtype)


def _attn_prompt(layer, q_t, k_t, v_t, bias_tiles, far_parts):
    b, aw, s = q_t.shape
    nb = s // MOBA_BLOCK
    ng = aw // LANES
    tq = Q_BLOCKS * MOBA_BLOCK
    assert nb <= SLOTS and nb % Q_BLOCKS == 0 and HEADS_PER_GROUP == 2
    rounds = min(MOBA_TOPK, nb)
    grid_spec = pltpu.PrefetchScalarGridSpec(
        num_scalar_prefetch=1,
        grid=(b, ng, s // tq),
        in_specs=[pl.BlockSpec((None, LANES, tq), lambda bi, g, i, far: (bi, g, i)),
                  pl.BlockSpec((None, None, LANES, s), lambda bi, g, i, far: (layer, bi, g, 0)),
                  pl.BlockSpec((None, None, LANES, s), lambda bi, g, i, far: (layer, bi, g, 0)),
                  pl.BlockSpec((HEADS_PER_GROUP,) + bias_tiles.shape[1:],
                               lambda bi, g, i, far: (g, 0, 0))],
        out_specs=pl.BlockSpec((None, tq, LANES), lambda bi, g, i, far: (bi, i, g)),
        scratch_shapes=[pltpu.VMEM((HEADS_PER_GROUP, nb + N_PAD, MOBA_BLOCK, LANES), BF16),
                        pltpu.VMEM((nb + N_PAD, LANES, MOBA_BLOCK), BF16),
                        pltpu.VMEM((LANES, LANES), F32)],
    )
    return pl.pallas_call(
        functools.partial(_attn_prompt_kernel, rounds, nb),
        grid_spec=grid_spec,
        out_shape=jax.ShapeDtypeStruct((b, s, aw), BF16),
        compiler_params=_params(3),
        name="moba_prompt",
    )(far_parts, q_t, k_t, v_t, bias_tiles)


def _attn_sample_kernel(n_pages, page, rounds, pt_ref, q_ref, kn_ref, vn_ref, bprev_ref, bown_ref,
                        far_ref, *rest):
    kp_refs = rest[:n_pages]
    vp_refs = rest[n_pages:2 * n_pages]
    o_ref, = rest[2 * n_pages:]
    t, aw = q_ref.shape
    rows = t * N_HEADS
    ppb = MOBA_BLOCK // page
    n_past = n_pages // ppb

    r_i = lax.broadcasted_iota(jnp.int32, (rows, aw), 0)
    c_i = lax.broadcasted_iota(jnp.int32, (rows, aw), 1)
    in_head = (c_i // HEAD_DIM) == (r_i % N_HEADS)
    q = q_ref[...]
    q_rep = jnp.zeros((rows, aw), F32)
    for qi in range(t):
        q_rep = jnp.where(r_i // N_HEADS == qi, q[qi:qi + 1, :], q_rep)
    qs = (jnp.where(in_head, q_rep, 0.0) * (HEAD_DIM ** -0.5)).astype(BF16)

    raw = [jnp.concatenate([_dot(qs, kp_refs[n * ppb + j][...].astype(BF16)) for j in range(ppb)],
                           axis=-1) for n in range(n_past)]
    gsum = [jnp.sum(r, axis=-1, keepdims=True) for r in raw]
    s_blocks = []
    for n in range(n_past):
        rank = jnp.zeros((rows, 1), F32)
        for m in range(n_past):
            if m != n:
                tie = 1.0 if m < n else 0.0
                rank = rank + jnp.where(gsum[m] > gsum[n], 1.0,
                                        jnp.where(gsum[m] == gsum[n], tie, 0.0))
        s_n = raw[n] + (bprev_ref[...] if n == n_past - 1 else far_ref[...])
        s_blocks.append(jnp.where(rank < rounds, s_n, NEG))
    pad = jnp.zeros((LANES - t, aw), F32)
    kn = jnp.concatenate([kn_ref[...], pad], axis=0).astype(BF16)
    vn = jnp.concatenate([vn_ref[...], pad], axis=0).astype(BF16)
    s_own = _dot_nt(qs, kn) + bown_ref[...]

    m = jnp.max(s_own, axis=-1, keepdims=True)
    for s_n in s_blocks:
        m = jnp.maximum(m, jnp.max(s_n, axis=-1, keepdims=True))
    p_own = jnp.exp(s_own - m)
    l = jnp.sum(p_own, axis=-1, keepdims=True)
    out = _dot(p_own.astype(BF16), vn)
    for n, s_n in enumerate(s_blocks):
        p_n = jnp.exp(s_n - m)
        l = l + jnp.sum(p_n, axis=-1, keepdims=True)
        p_n = p_n.astype(BF16)
        for j in range(ppb):
            out = out + _dot_nt(p_n[:, j * page:(j + 1) * page], vp_refs[n * ppb + j][...].astype(BF16))
    out = jnp.where(in_head, out / l, 0.0)
    o_ref[...] = jnp.sum(out.reshape(t, N_HEADS, aw), axis=1)


def _attn_sample(layer, q, k_new, v_new, cache_kt, cache_vt, page_table, bprev, bown, far):
    n, t, aw = q.shape
    n_pages = page_table.shape[1]
    page = cache_kt.shape[3]
    n_past_tokens = n_pages * page
    assert page == LANES and MOBA_BLOCK % page == 0 and n_past_tokens % MOBA_BLOCK == 0
    assert t <= LANES and n_past_tokens // MOBA_BLOCK <= LANES
    n_blocks = n_past_tokens // MOBA_BLOCK + 1
    rounds = min(MOBA_TOPK, n_blocks)
    tok = pl.BlockSpec((None, t, aw), lambda i, pt: (i, 0, 0))
    const2 = lambda a: pl.BlockSpec(a.shape, lambda i, pt: (0, 0))
    page_specs = [pl.BlockSpec((None, None, aw, page), lambda i, pt, p=p: (layer, pt[i, p], 0, 0))
                  for p in range(n_pages)]
    grid_spec = pltpu.PrefetchScalarGridSpec(
        num_scalar_prefetch=1,
        grid=(n,),
        in_specs=[tok, tok, tok, const2(bprev), const2(bown), const2(far)] + page_specs + page_specs,
        out_specs=pl.BlockSpec((None, t, aw), lambda i, pt: (i, 0, 0)),
    )
    return pl.pallas_call(
        functools.partial(_attn_sample_kernel, n_pages, page, rounds),
        grid_spec=grid_spec,
        out_shape=jax.ShapeDtypeStruct((n, t, aw), F32),
        compiler_params=_params(1),
        name="moba_sample",
    )(page_table, q, k_new, v_new, bprev, bown, far, *([cache_kt] * n_pages), *([cache_vt] * n_pages))


def _mixer_out_kernel(x_ref, gt_ref, c_ref, a_ref, sga_ref, sgb_ref, wc_ref, wa_ref, wo_ref, o_ref):
    conv_y = _dot(c_ref[...].astype(BF16), wc_ref[...])
    attn_y = _dot(a_ref[...].astype(BF16), wa_ref[...])
    mixed = (sga_ref[...].astype(F32) * conv_y + sgb_ref[...].astype(F32) * attn_y).astype(BF16)
    o_ref[...] = x_ref[...] + gt_ref[...] * _dot(mixed, wo_ref[...])


def _mixer_out(x, gt, cact, att, sga, sgb, wc, wa, wo, layer, tiles_per_seq):
    nt, d = x.shape
    tm = min(TOKEN_TILE, nt)
    row = lambda i: (i, 0)
    return pl.pallas_call(
        _mixer_out_kernel,
        grid=(nt // tm,),
        in_specs=[pl.BlockSpec((tm, d), row), _mod_spec(gt, tm, tiles_per_seq),
                  pl.BlockSpec((tm, cact.shape[1]), row), pl.BlockSpec((tm, att.shape[1]), row),
                  pl.BlockSpec((tm, d), row), pl.BlockSpec((tm, d), row),
                  _stacked_spec(wc, (layer,)), _stacked_spec(wa, (layer,)),
                  _stacked_spec(wo, (layer,))],
        out_specs=pl.BlockSpec((tm, d), row),
        out_shape=jax.ShapeDtypeStruct((nt, d), F32),
        compiler_params=_params(1),
        name="mixer_out",
    )(x, gt, cact, att, sga, sgb, wc, wa, wo)


def kernel(x_prompt, x_sample, c_prompt, c_sample, cache_k, cache_v, state_conv, page_table, rel_bias, w_ada, b_ada, g_norm, w_ffn_gate, w_ffn_up, w_ffn_down, w_in, conv_w, conv_b, conv_ln_g, conv_ln_b, w_conv_out, w_attn_out, w_o, g_final):
    b, s, d = x_prompt.shape
    n, t, _ = x_sample.shape
    depth = w_ada.shape[0]
    cw = conv_w.shape[2]
    aw = N_HEADS * HEAD_DIM
    d_ff = w_ffn_gate.shape[-1]

    nc = -(-(b + n) // 8) * 8
    c_all = jnp.concatenate([c_prompt, c_sample, jnp.zeros((nc - b - n, d), F32)], axis=0)
    mod = _ada(c_all, w_ada, b_ada)
    assert t <= LANES
    tiles = _bias_tiles(rel_bias, 1.0, LANES)
    tiles_log2 = _bias_tiles(rel_bias, LOG2E, Q_BLOCKS * MOBA_BLOCK)
    far_bias = rel_bias[:, N_BUCKETS - 1]
    far_log2 = far_bias * LOG2E
    far_hi = far_log2.astype(BF16).astype(F32)
    far_parts = jnp.stack([far_hi, far_log2 - far_hi])

    rows = t * N_HEADS
    own_key = N_PAD * MOBA_BLOCK
    bprev_s = tiles[:, own_key - MOBA_BLOCK:own_key, :t].transpose(2, 0, 1).reshape(rows, MOBA_BLOCK)
    bown_s = jnp.pad(tiles[:, own_key:own_key + t, :t].transpose(2, 0, 1).reshape(rows, t),
                     ((0, 0), (0, LANES - t)), constant_values=NEG)
    far_s = jnp.broadcast_to(jnp.tile(far_bias, t)[:, None], (rows, MOBA_BLOCK))

    cache_kt = cache_k.transpose(0, 1, 3, 4, 2).reshape(cache_k.shape[:2] + (aw, cache_k.shape[2]))
    cache_vt = cache_v.transpose(0, 1, 3, 4, 2).reshape(cache_v.shape[:2] + (aw, cache_v.shape[2]))
    state_t = state_conv.transpose(0, 2, 1, 3)
    hist = state_t.shape[1]

    assert d_ff % MXU_DIM == 0
    wg, wu, wd = w_ffn_gate.astype(BF16), w_ffn_up.astype(BF16), w_ffn_down.astype(BF16)
    w_in_b = w_in.astype(BF16)
    w_qkv_t = w_in[:, :, 2 * cw:2 * cw + 3 * aw].transpose(0, 2, 1).astype(BF16)
    wc_b = w_conv_out.astype(BF16)
    wa_b = w_attn_out.astype(BF16)
    wo_b = w_o.astype(BF16)
    gf = g_final.reshape(1, d)

    def ffn(x, mm, l, i, tiles_per_seq, final):
        o = 6 * i
        return _ffn(x, mm[o], mm[o + 1], mm[o + 2], g_norm[l, 2 * i:2 * i + 1], wg, wu, wd, (l, i),
                    tiles_per_seq, gf if final else None)

    tm_p = min(TOKEN_TILE, s)
    assert s % tm_p == 0 and s % MOBA_BLOCK == 0
    tps = s // tm_p
    x = x_prompt.reshape(b * s, d)
    k_stack = jnp.zeros((depth, b, aw, s), F32)
    v_stack = jnp.zeros((depth, b, aw, s), F32)
    cp = []
    for l in range(depth):
        mm = [mod[l, i, :b][:, None, :] for i in range(N_MOD)]
        x = ffn(x, mm, l, 0, tps, False)
        u, q_t, k_stack, v_stack, sga, sgb = _mixer_in(x, mm[3], mm[4], g_norm[l, 1:2], w_in_b, l, cw,
                                                       aw, tps, w_qkv_t, (k_stack, v_stack))
        u3 = u.reshape(b, s, cw)
        cact = _conv_prompt(u3, conv_w[l], conv_b[l], conv_ln_g[l], conv_ln_b[l])
        att = _attn_prompt(l, q_t, k_stack, v_stack, tiles_log2, far_parts)
        x = _mixer_out(x, mm[5], cact.reshape(b * s, cw), att.reshape(b * s, aw), sga, sgb,
                       wc_b, wa_b, wo_b, l, tps)
        x = ffn(x, mm, l, 1, tps, l == depth - 1)
        cp.append(u3[:, s - hist:])
    y_p = x.reshape(b, s, d)
    to_cache = lambda z: z.reshape(depth, b, N_HEADS, HEAD_DIM, s).transpose(0, 1, 4, 2, 3)
    k_p, v_p, c_p = to_cache(k_stack), to_cache(v_stack), jnp.stack(cp)

    x = x_sample.reshape(n * t, d)
    ks, vs, cs = [], [], []
    for l in range(depth):
        mm = [jnp.repeat(mod[l, i, b:b + n], t, axis=0) for i in range(N_MOD)]
        x = ffn(x, mm, l, 0, 1, False)
        u, q, k, v, sga, sgb = _mixer_in(x, mm[3], mm[4], g_norm[l, 1:2], w_in_b, l, cw, aw, 1)
        u_t = u.reshape(n, t, cw).transpose(1, 0, 2)
        cact_t, new_state = _conv_sample(l, state_t, u_t, conv_w[l], conv_b[l], conv_ln_g[l],
                                         conv_ln_b[l])
        att = _attn_sample(l, q.reshape(n, t, aw), k.reshape(n, t, aw), v.reshape(n, t, aw),
                           cache_kt, cache_vt, page_table, bprev_s, bown_s, far_s)
        x = _mixer_out(x, mm[5], cact_t.transpose(1, 0, 2).reshape(n * t, cw), att.reshape(n * t, aw),
                       sga, sgb, wc_b, wa_b, wo_b, l, 1)
        x = ffn(x, mm, l, 1, 1, l == depth - 1)
        ks.append(k.reshape(n, t, N_HEADS, HEAD_DIM))
        vs.append(v.reshape(n, t, N_HEADS, HEAD_DIM))
        cs.append(new_state)
    y_s = x.reshape(n, t, d)
    return (y_p, y_s, k_p, v_p, c_p, jnp.stack(ks), jnp.stack(vs),
            jnp.stack(cs).transpose(0, 2, 1, 3))
```

```python
import functools
import math

import jax
import jax.numpy as jnp
from jax import lax
from jax.experimental import pallas as pl
from jax.experimental.pallas import tpu as pltpu

N_HEADS = 8
HEAD_DIM = 64
MOBA_BLOCK = 256
MOBA_TOPK = 3
N_BUCKETS = 32
MAX_DISTANCE = 128
N_MOD = 9
HALF = 0.5
EPS = 1e-6
NEG = -1e30
LOG2E = math.log2(math.e)

LANES = 128
SUBLANES = 8
HEADS_PER_GROUP = LANES // HEAD_DIM
MXU_DIM = 256
TOKEN_TILE = 512
VMEM_LIMIT = 56 * 1024 * 1024

F32 = jnp.float32
BF16 = jnp.bfloat16


def _params(n_grid_dims):
    return pltpu.CompilerParams(dimension_semantics=("arbitrary",) * n_grid_dims,
                                vmem_limit_bytes=VMEM_LIMIT)


def _dot(a, b):
    return jnp.dot(a, b, preferred_element_type=F32)


def _dot_nt(a, b, precision=None):
    return lax.dot_general(a, b, (((1,), (1,)), ((), ())), precision=precision,
                           preferred_element_type=F32)


def _silu(x):
    return x * jax.nn.sigmoid(x)


def _modulated_norm(x, g, shift, scale):
    y = x * lax.rsqrt(jnp.mean(x * x, axis=-1, keepdims=True) + EPS) * g
    return y * (1.0 + scale) + shift


def _ada_kernel(c_ref, w_ref, b_ref, o_ref):
    a = _silu(c_ref[...]).astype(BF16)
    o_ref[...] = _dot(a, w_ref[...].astype(BF16)) + b_ref[...]


def _ada(c_all, w_ada, b_ada):
    depth, d, nd = w_ada.shape
    nc = c_all.shape[0]
    assert nd == N_MOD * d and d % LANES == 0
    return pl.pallas_call(
        _ada_kernel,
        grid=(depth, N_MOD),
        in_specs=[pl.BlockSpec((nc, d), lambda l, j: (0, 0)),
                  pl.BlockSpec((None, d, d), lambda l, j: (l, 0, j)),
                  pl.BlockSpec((None, 1, d), lambda l, j: (l, 0, j))],
        out_specs=pl.BlockSpec((None, None, nc, d), lambda l, j: (l, j, 0, 0)),
        out_shape=jax.ShapeDtypeStruct((depth, N_MOD, nc, d), F32),
        compiler_params=_params(2),
        name="ada_mod",
    )(c_all, w_ada, b_ada.reshape(depth, 1, nd))


Q_BLOCKS = 2
NEAR_BLOCKS = 4


def _bias_kernel(scale, rel_ref, o_ref):
    h = pl.program_id(0)
    shape = o_ref.shape
    key = lax.broadcasted_iota(jnp.int32, shape, 0)
    qry = lax.broadcasted_iota(jnp.int32, shape, 1)
    d = qry - key + (NEAR_BLOCKS - Q_BLOCKS) * MOBA_BLOCK
    n = jnp.maximum(d, 0)
    max_exact = N_BUCKETS // 2
    nf = jnp.maximum(n, 1).astype(F32)
    large = max_exact + (jnp.log(nf / max_exact) / math.log(MAX_DISTANCE / max_exact)
                         * (N_BUCKETS - max_exact)).astype(jnp.int32)
    large = jnp.minimum(large, N_BUCKETS - 1)
    bucket = jnp.where(n < max_exact, n, large)
    bias = jnp.zeros(shape, F32)
    for b in range(N_BUCKETS):
        bias = jnp.where(bucket == b, rel_ref[h, b], bias)
    o_ref[...] = jnp.where(d < 0, NEG, bias * scale)


def _bias_tiles(rel_bias, scale, n_query):
    shape = (NEAR_BLOCKS * MOBA_BLOCK, n_query)
    return pl.pallas_call(
        functools.partial(_bias_kernel, scale),
        grid=(N_HEADS,),
        in_specs=[pl.BlockSpec(memory_space=pltpu.SMEM)],
        out_specs=pl.BlockSpec((None,) + shape, lambda h: (h, 0, 0)),
        out_shape=jax.ShapeDtypeStruct((N_HEADS,) + shape, F32),
        compiler_params=_params(1),
        name="t5_bias_tiles",
    )(rel_bias)


def _mod_spec(m, tm, tiles_per_seq):
    if m.ndim == 3:
        return pl.BlockSpec((None, 1, m.shape[-1]), lambda i: (i // tiles_per_seq, 0, 0))
    return pl.BlockSpec((tm, m.shape[-1]), lambda i: (i, 0))


def _const_spec(a):
    zeros = (0,) * a.ndim
    return pl.BlockSpec(a.shape, lambda i: zeros)


def _stacked_spec(a, prefix):
    idx = tuple(prefix) + (0, 0)
    return pl.BlockSpec((None,) * len(prefix) + a.shape[-2:], lambda i: idx)


def _ffn_kernel(final_norm, x_ref, sh_ref, sc_ref, gt_ref, g_ref, wg_ref, wu_ref, wd_ref, *rest):
    if final_norm:
        gf_ref, o_ref, h_scr, acc_scr = rest
    else:
        o_ref, h_scr, acc_scr = rest
    x = x_ref[...]
    h_scr[...] = _modulated_norm(x, g_ref[...], sh_ref[...], sc_ref[...]).astype(BF16)
    acc_scr[...] = jnp.zeros_like(acc_scr)

    for c in range(wg_ref.shape[1] // MXU_DIM):
        cols = slice(c * MXU_DIM, (c + 1) * MXU_DIM)
        h = h_scr[...]
        g = _dot(h, wg_ref[:, cols])
        u = _dot(h, wu_ref[:, cols])
        acc_scr[...] += _dot((_silu(g) * u).astype(BF16), wd_ref[cols, :])
    y = x + (HALF * gt_ref[...]) * acc_scr[...]
    if final_norm:
        y = y * lax.rsqrt(jnp.mean(y * y, axis=-1, keepdims=True) + EPS) * gf_ref[...]
    o_ref[...] = y


def _ffn(x, sh, sc, gt, g, wg, wu, wd, widx, tiles_per_seq, g_final=None):
    nt, d = x.shape
    tm = min(TOKEN_TILE, nt)
    args = [x, sh, sc, gt, g, wg, wu, wd]
    in_specs = [pl.BlockSpec((tm, d), lambda i: (i, 0)),
                _mod_spec(sh, tm, tiles_per_seq), _mod_spec(sc, tm, tiles_per_seq),
                _mod_spec(gt, tm, tiles_per_seq),
                _const_spec(g), _stacked_spec(wg, widx), _stacked_spec(wu, widx),
                _stacked_spec(wd, widx)]
    if g_final is not None:
        args.append(g_final)
        in_specs.append(_const_spec(g_final))
    return pl.pallas_call(
        functools.partial(_ffn_kernel, g_final is not None),
        grid=(nt // tm,),
        in_specs=in_specs,
        out_specs=pl.BlockSpec((tm, d), lambda i: (i, 0)),
        out_shape=jax.ShapeDtypeStruct((nt, d), F32),
        scratch_shapes=[pltpu.VMEM((tm, d), BF16), pltpu.VMEM((tm, d), F32)],
        compiler_params=_params(1),
        name="swiglu_ffn",
    )(*args)


def _mixer_in_kernel(transposed_qkv, cw, aw, x_ref, sh_ref, sc_ref, g_ref, w_ref, *rest):
    if transposed_qkv:
        wqkv_ref, _, _, u_ref, q_ref, k_ref, v_ref, sga_ref, sgb_ref = rest
    else:
        u_ref, q_ref, k_ref, v_ref, sga_ref, sgb_ref = rest
    d = x_ref.shape[-1]
    h = _modulated_norm(x_ref[...], g_ref[...], sh_ref[...], sc_ref[...]).astype(BF16)
    a = _dot(h, w_ref[:, 0:cw])
    g = _dot(h, w_ref[:, cw:2 * cw])
    u_ref[...] = a * jax.nn.sigmoid(g)
    o = 2 * cw
    if transposed_qkv:
        q_ref[...] = _dot_nt(wqkv_ref[0:aw, :], h)
        k_ref[...] = _dot_nt(wqkv_ref[aw:2 * aw, :], h)
        v_ref[...] = _dot_nt(wqkv_ref[2 * aw:3 * aw, :], h)
    else:
        q_ref[...] = _dot(h, w_ref[:, o:o + aw])
        k_ref[...] = _dot(h, w_ref[:, o + aw:o + 2 * aw])
        v_ref[...] = _dot(h, w_ref[:, o + 2 * aw:o + 3 * aw])
    o += 3 * aw
    sga_ref[...] = jax.nn.sigmoid(_dot(h, w_ref[:, o:o + d])).astype(sga_ref.dtype)
    sgb_ref[...] = jax.nn.sigmoid(_dot(h, w_ref[:, o + d:o + 2 * d])).astype(sgb_ref.dtype)


def _mixer_in(x, sh, sc, g, w_in, layer, cw, aw, tiles_per_seq, w_qkv_t=None, kv_stack=None):
    nt, d = x.shape
    tm = min(TOKEN_TILE, nt)
    row = lambda i: (i, 0)
    transposed_qkv = w_qkv_t is not None
    args = [x, sh, sc, g, w_in]
    in_specs = [pl.BlockSpec((tm, d), row), _mod_spec(sh, tm, tiles_per_seq),
                _mod_spec(sc, tm, tiles_per_seq), _const_spec(g), _stacked_spec(w_in, (layer,))]
    aliases = {}
    if transposed_qkv:
        nseq = nt // (tiles_per_seq * tm)
        q_shape = jax.ShapeDtypeStruct((nseq, aw, tiles_per_seq * tm), F32)
        q_spec = pl.BlockSpec((None, aw, tm), lambda i: (i // tiles_per_seq, 0, i % tiles_per_seq))
        kv_shape = jax.ShapeDtypeStruct(kv_stack[0].shape, F32)
        kv_spec = pl.BlockSpec((None, None, aw, tm),
                               lambda i: (layer, i // tiles_per_seq, 0, i % tiles_per_seq))
        args += [w_qkv_t, kv_stack[0], kv_stack[1]]
        in_specs += [_stacked_spec(w_qkv_t, (layer,)), pl.BlockSpec(memory_space=pl.ANY),
                     pl.BlockSpec(memory_space=pl.ANY)]
        aliases = {len(args) - 2: 2, len(args) - 1: 3}
    else:
        q_shape = kv_shape = jax.ShapeDtypeStruct((nt, aw), F32)
        q_spec = kv_spec = pl.BlockSpec((tm, aw), row)
    out_shape = [jax.ShapeDtypeStruct((nt, cw), F32), q_shape, kv_shape, kv_shape] \
        + [jax.ShapeDtypeStruct((nt, d), BF16)] * 2
    out_specs = [pl.BlockSpec((tm, cw), row), q_spec, kv_spec, kv_spec] \
        + [pl.BlockSpec((tm, d), row)] * 2
    return pl.pallas_call(
        functools.partial(_mixer_in_kernel, transposed_qkv, cw, aw),
        grid=(nt // tm,),
        in_specs=in_specs,
        out_specs=out_specs,
        out_shape=out_shape,
        input_output_aliases=aliases,
        compiler_params=_params(1),
        name="mixer_in_proj",
    )(*args)


def _ln_swish(cv, g, b):
    mu = jnp.mean(cv, axis=-1, keepdims=True)
    xc = cv - mu
    y = xc * lax.rsqrt(jnp.mean(xc * xc, axis=-1, keepdims=True) + EPS) * g + b
    return _silu(y)


HALO = 32
CONV_ROWS = 64


def _conv_prompt_kernel(conv_w, cur_ref, halo_ref, w_ref, b_ref, g_ref, bb_ref, o_ref, ext_scr):
    i = pl.program_id(1)
    tt = cur_ref.shape[0]
    n_ext = HALO + tt
    halo = halo_ref[...]
    ext_scr[0, 0:HALO, :] = jnp.where(i > 0, halo, jnp.zeros_like(halo))
    ext_scr[0, HALO:n_ext, :] = cur_ref[...]
    for s in range(1, SUBLANES):
        ext_scr[s, 0:n_ext - s, :] = ext_scr[0, s:n_ext, :]
    base = HALO - (conv_w - 1)
    for r in range(tt // CONV_ROWS):
        acc = jnp.zeros((CONV_ROWS, cur_ref.shape[1]), F32)
        for j in range(conv_w):
            off = r * CONV_ROWS + base + j
            lo = off - off % SUBLANES
            acc = acc + w_ref[j:j + 1, :] * ext_scr[off % SUBLANES, lo:lo + CONV_ROWS, :]
        y = _ln_swish(acc + b_ref[...], g_ref[...], bb_ref[...])
        o_ref[r * CONV_ROWS:(r + 1) * CONV_ROWS, :] = y.astype(o_ref.dtype)


def _conv_prompt(u, conv_w, conv_b, ln_g, ln_b):
    b, s, c = u.shape
    cw = conv_w.shape[0]
    assert cw - 1 <= HALO
    tt = min(TOKEN_TILE, s)
    hb = tt // HALO
    vec = lambda a: a.reshape(1, c)
    const = lambda bi, i: (0, 0)
    return pl.pallas_call(
        functools.partial(_conv_prompt_kernel, cw),
        grid=(b, s // tt),
        in_specs=[pl.BlockSpec((None, tt, c), lambda bi, i: (bi, i, 0)),
                  pl.BlockSpec((None, HALO, c), lambda bi, i: (bi, jnp.maximum(i * hb - 1, 0), 0)),
                  pl.BlockSpec((cw, c), const), pl.BlockSpec((1, c), const),
                  pl.BlockSpec((1, c), const), pl.BlockSpec((1, c), const)],
        out_specs=pl.BlockSpec((None, tt, c), lambda bi, i: (bi, i, 0)),
        out_shape=jax.ShapeDtypeStruct((b, s, c), BF16),
        scratch_shapes=[pltpu.VMEM((SUBLANES, HALO + tt, c), F32)],
        compiler_params=_params(2),
        name="conv_prompt",
    )(u, u, conv_w, vec(conv_b), vec(ln_g), vec(ln_b))


SEQ_CHUNK = 32


def _conv_sample_kernel(hist, st_ref, u_ref, w_ref, b_ref, g_ref, bb_ref, o_ref, ns_ref):
    t_new = u_ref.shape[0]
    row = lambda r: st_ref[r] if r < hist else u_ref[r - hist]
    for t in range(t_new):
        acc = jnp.zeros(u_ref.shape[1:], F32)
        for j in range(w_ref.shape[0]):
            acc = acc + w_ref[j:j + 1, :] * row(t + j)
        o_ref[t] = _ln_swish(acc + b_ref[...], g_ref[...], bb_ref[...])
    for r in range(hist):
        ns_ref[r] = row(r + t_new)


def _conv_sample(layer, state_t, u_t, conv_w, conv_b, ln_g, ln_b):
    _, hist, n, c = state_t.shape
    t = u_t.shape[0]
    assert conv_w.shape[0] == hist + 1
    sc = min(SEQ_CHUNK, n)
    vec = lambda a: a.reshape(1, c)
    const = lambda i: (0, 0)
    return pl.pallas_call(
        functools.partial(_conv_sample_kernel, hist),
        grid=(n // sc,),
        in_specs=[pl.BlockSpec((None, hist, sc, c), lambda i: (layer, 0, i, 0)),
                  pl.BlockSpec((t, sc, c), lambda i: (0, i, 0)),
                  pl.BlockSpec(conv_w.shape, const), pl.BlockSpec((1, c), const),
                  pl.BlockSpec((1, c), const), pl.BlockSpec((1, c), const)],
        out_specs=[pl.BlockSpec((t, sc, c), lambda i: (0, i, 0)),
                   pl.BlockSpec((hist, sc, c), lambda i: (0, i, 0))],
        out_shape=[jax.ShapeDtypeStruct((t, n, c), F32), jax.ShapeDtypeStruct((hist, n, c), F32)],
        compiler_params=_params(1),
        name="conv_sample",
    )(state_t, u_t, conv_w, vec(conv_b), vec(ln_g), vec(ln_b))


SLOTS = 16
PAD_SLOT = 3 * SLOTS
N_PAD = NEAR_BLOCKS - Q_BLOCKS
FAR_GROUP = 4
NEG_BIG = -(2.0 ** 100)


def _attn_prompt_kernel(rounds, nb, *refs):
    for i in range(nb // Q_BLOCKS):
        pl.when(pl.program_id(2) == i)(functools.partial(_attn_prompt_tile, i, rounds, nb, *refs))


def _attn_prompt_tile(i, rounds, nb, far_ref, qt_ref, kt_ref, vt_ref, bias_ref, o_ref,
                      kx_scr, vx_scr, kmx_scr):
    hp = pl.program_id(1)
    blk = MOBA_BLOCK
    tq = Q_BLOCKS * blk

    def fill_scratch():
        lane = lax.broadcasted_iota(jnp.int32, (blk, LANES), 1)
        jrel = lane % HEAD_DIM
        r2 = lax.broadcasted_iota(jnp.int32, (LANES, LANES), 0)
        c2 = lax.broadcasted_iota(jnp.int32, (LANES, LANES), 1)
        kmcols = jnp.zeros((LANES, LANES), F32)
        for p in range(N_PAD):
            vx_scr[p] = jnp.zeros((LANES, blk), BF16)
            for hh in range(HEADS_PER_GROUP):
                spare = (lane // HEAD_DIM) != hh
                kx_scr[hh, p] = jnp.where(spare & (jrel == PAD_SLOT), 1.0, 0.0).astype(BF16)
        for n in range(nb):
            kt_blk = kt_ref[:, n * blk:(n + 1) * blk]
            vx_scr[n + N_PAD] = vt_ref[:, n * blk:(n + 1) * blk].astype(BF16)
            kmcols = jnp.where(c2 == n, jnp.sum(kt_blk, axis=1, keepdims=True) * (1.0 / blk), kmcols)
            k_blk = kt_blk.T
            ind = jnp.where((jrel < PAD_SLOT) & (jrel % SLOTS == n), 1.0, 0.0)
            for hh in range(HEADS_PER_GROUP):
                spare = (lane // HEAD_DIM) != hh
                kx_scr[hh, n + N_PAD] = jnp.where(spare, ind, k_blk).astype(BF16)
        pick = jnp.where((r2 % HEAD_DIM) == c2, 1.0, 0.0)
        pick = jnp.where((r2 % HEAD_DIM) < SLOTS, pick, 0.0)
        kmx = _dot_nt(pick, kmcols, precision=lax.Precision.HIGHEST)
        kmx_scr[...] = jnp.where((r2 // HEAD_DIM) != (c2 // HEAD_DIM), kmx, 0.0)

    if i == 0:
        fill_scratch()

    qt = qt_ref[...]
    gate_t = jnp.dot(kmx_scr[...], qt, precision=lax.Precision.HIGHEST,
                     preferred_element_type=F32)
    n_i = lax.broadcasted_iota(jnp.int32, (SLOTS, tq), 0)
    q_i = lax.broadcasted_iota(jnp.int32, (SLOTS, tq), 1)
    own = Q_BLOCKS * i + q_i // blk
    first_near = Q_BLOCKS * i - N_PAD
    zeros = jnp.zeros((SLOTS, tq), F32)
    groups = []
    for hh in range(HEADS_PER_GROUP):
        base = (1 - hh) * HEAD_DIM
        g = jnp.where(n_i < own, gate_t[base:base + SLOTS, :], NEG)
        rank = zeros
        for m in range(nb):
            gm = g[m:m + 1, :]
            tie = jnp.where(n_i > m, 1.0, 0.0)
            rank = rank + jnp.where(gm > g, 1.0, jnp.where(gm == g, tie, 0.0))
        sel = jnp.where(n_i < own, jnp.where(rank < rounds, 1.0, 0.0), 0.0) > 0.5
        is_far = n_i < first_near
        head = hp * HEADS_PER_GROUP + hh
        far_hi = jnp.where(is_far, jnp.where(sel, far_ref[0, head], NEG_BIG), 0.0)
        far_lo = jnp.where(is_far, jnp.where(sel, far_ref[1, head], 0.0), 0.0)
        near = jnp.where(is_far, 0.0, jnp.where(n_i == own, 0.0, jnp.where(sel, 0.0, NEG_BIG)))
        pad = jnp.where(n_i == 0, NEG_BIG, 0.0)
        groups.append(jnp.concatenate([far_hi, far_lo, near, pad], axis=0))
    spare_vals = jnp.concatenate(groups[::-1], axis=0)
    chan = lax.broadcasted_iota(jnp.int32, qt.shape, 0)
    qs = qt * (HEAD_DIM ** -0.5 * LOG2E)

    def attend(hh, qx, first, n_blocks, bias):
        ss = []
        for j in range(n_blocks):
            s = _dot(kx_scr[hh, first + j], qx)
            if bias is not None:
                s = s + bias_ref[hh, j * blk:(j + 1) * blk, :]
            ss.append(s)
        m = jnp.max(ss[0], axis=0, keepdims=True)
        for s in ss[1:]:
            m = jnp.maximum(m, jnp.max(s, axis=0, keepdims=True))
        return ss, m

    def weigh(hh, ss, m, first):
        rows = slice(hh * HEAD_DIM, (hh + 1) * HEAD_DIM)
        l, acc = None, None
        for j, s in enumerate(ss):
            p = jnp.exp2(s - m)
            lj = jnp.sum(p, axis=0, keepdims=True)
            aj = _dot(vx_scr[first + j][rows, :], p.astype(BF16))
            l = lj if l is None else l + lj
            acc = aj if acc is None else acc + aj
        return l, acc

    heads = range(HEADS_PER_GROUP)
    qx = [jnp.where((chan // HEAD_DIM) == hh, qs, spare_vals).astype(BF16) for hh in heads]
    first = Q_BLOCKS * i
    scored = [attend(hh, qx[hh], first, NEAR_BLOCKS, True) for hh in heads]
    carry = [(scored[hh][1],) + weigh(hh, scored[hh][0], scored[hh][1], first) for hh in heads]

    def far(first, n_blocks, carry):
        out = []
        scored = [attend(hh, qx[hh], first, n_blocks, None) for hh in heads]
        for hh in heads:
            m, l, acc = carry[hh]
            ss, m_blk = scored[hh]
            m_new = jnp.maximum(m, m_blk)
            alpha = jnp.exp2(m - m_new)
            l_blk, acc_blk = weigh(hh, ss, m_new, first)
            out.append((m_new, alpha * l + l_blk, alpha * acc + acc_blk))
        return tuple(out)

    for start in range(0, max(first_near, 0), FAR_GROUP):
        carry = far(N_PAD + start, min(FAR_GROUP, first_near - start), carry)
    out_t = jnp.concatenate([acc / l for _, l, acc in carry], axis=0)
    o_ref[...] = out_t.T.astype(o_ref.dtype)


def _attn_prompt(layer, q_t, k_t, v_t, bias_tiles, far_parts):
    b, aw, s = q_t.shape
    nb = s // MOBA_BLOCK
    ng = aw // LANES
    tq = Q_BLOCKS * MOBA_BLOCK
    assert nb <= SLOTS and nb % Q_BLOCKS == 0 and HEADS_PER_GROUP == 2
    rounds = min(MOBA_TOPK, nb)
    grid_spec = pltpu.PrefetchScalarGridSpec(
        num_scalar_prefetch=1,
        grid=(b, ng, s // tq),
        in_specs=[pl.BlockSpec((None, LANES, tq), lambda bi, g, i, far: (bi, g, i)),
                  pl.BlockSpec((None, None, LANES, s), lambda bi, g, i, far: (layer, bi, g, 0)),
                  pl.BlockSpec((None, None, LANES, s), lambda bi, g, i, far: (layer, bi, g, 0)),
                  pl.BlockSpec((HEADS_PER_GROUP,) + bias_tiles.shape[1:],
                               lambda bi, g, i, far: (g, 0, 0))],
        out_specs=pl.BlockSpec((None, tq, LANES), lambda bi, g, i, far: (bi, i, g)),
        scratch_shapes=[pltpu.VMEM((HEADS_PER_GROUP, nb + N_PAD, MOBA_BLOCK, LANES), BF16),
                        pltpu.VMEM((nb + N_PAD, LANES, MOBA_BLOCK), BF16),
                        pltpu.VMEM((LANES, LANES), F32)],
    )
    return pl.pallas_call(
        functools.partial(_attn_prompt_kernel, rounds, nb),
        grid_spec=grid_spec,
        out_shape=jax.ShapeDtypeStruct((b, s, aw), BF16),
        compiler_params=_params(3),
        name="moba_prompt",
    )(far_parts, q_t, k_t, v_t, bias_tiles)


SEQ_PAIR = 2


def _attn_sample_kernel(n_pages, page, rounds, pt_ref, q_ref, kn_ref, vn_ref, bprev_ref, bown_ref,
                        far_ref, *rest):
    n_seq = q_ref.shape[0]
    o_ref = rest[2 * n_seq * n_pages]
    for sq in range(n_seq):
        kp_refs = rest[sq * n_pages:(sq + 1) * n_pages]
        vp_refs = rest[(n_seq + sq) * n_pages:(n_seq + sq + 1) * n_pages]
        _attn_sample_one(n_pages, page, rounds, q_ref.at[sq], kn_ref.at[sq], vn_ref.at[sq], bprev_ref,
                         bown_ref, far_ref, kp_refs, vp_refs, o_ref.at[sq])


def _attn_sample_one(n_pages, page, rounds, q_ref, kn_ref, vn_ref, bprev_ref, bown_ref, far_ref,
                     kp_refs, vp_refs, o_ref):
    t, aw = q_ref.shape
    rows = t * N_HEADS
    ppb = MOBA_BLOCK // page
    n_past = n_pages // ppb

    r_i = lax.broadcasted_iota(jnp.int32, (rows, aw), 0)
    c_i = lax.broadcasted_iota(jnp.int32, (rows, aw), 1)
    in_head = (c_i // HEAD_DIM) == (r_i % N_HEADS)
    q = q_ref[...]
    q_rep = jnp.zeros((rows, aw), F32)
    for qi in range(t):
        q_rep = jnp.where(r_i // N_HEADS == qi, q[qi:qi + 1, :], q_rep)
    qs = (jnp.where(in_head, q_rep, 0.0) * (HEAD_DIM ** -0.5)).astype(BF16)

    raw = [jnp.concatenate([_dot(qs, kp_refs[n * ppb + j][...].astype(BF16)) for j in range(ppb)],
                           axis=-1) for n in range(n_past)]
    gsum = [jnp.sum(r, axis=-1, keepdims=True) for r in raw]
    s_blocks = []
    for n in range(n_past):
        rank = jnp.zeros((rows, 1), F32)
        for m in range(n_past):
            if m != n:
                tie = 1.0 if m < n else 0.0
                rank = rank + jnp.where(gsum[m] > gsum[n], 1.0,
                                        jnp.where(gsum[m] == gsum[n], tie, 0.0))
        s_n = raw[n] + (bprev_ref[...] if n == n_past - 1 else far_ref[...])
        s_blocks.append(jnp.where(rank < rounds, s_n, NEG))
    pad = jnp.zeros((LANES - t, aw), F32)
    kn = jnp.concatenate([kn_ref[...], pad], axis=0).astype(BF16)
    vn = jnp.concatenate([vn_ref[...], pad], axis=0).astype(BF16)
    s_own = _dot_nt(qs, kn) + bown_ref[...]

    m = jnp.max(s_own, axis=-1, keepdims=True)
    for s_n in s_blocks:
        m = jnp.maximum(m, jnp.max(s_n, axis=-1, keepdims=True))
    p_own = jnp.exp(s_own - m)
    l = jnp.sum(p_own, axis=-1, keepdims=True)
    out = _dot(p_own.astype(BF16), vn)
    for n, s_n in enumerate(s_blocks):
        p_n = jnp.exp(s_n - m)
        l = l + jnp.sum(p_n, axis=-1, keepdims=True)
        p_n = p_n.astype(BF16)
        for j in range(ppb):
            out = out + _dot_nt(p_n[:, j * page:(j + 1) * page], vp_refs[n * ppb + j][...].astype(BF16))
    out = jnp.where(in_head, out / l, 0.0)
    o_ref[...] = jnp.sum(out.reshape(t, N_HEADS, aw), axis=1)


def _attn_sample(layer, q, k_new, v_new, cache_kt, cache_vt, page_table, bprev, bown, far):
    n, t, aw = q.shape
    n_pages = page_table.shape[1]
    page = cache_kt.shape[3]
    n_past_tokens = n_pages * page
    assert page == LANES and MOBA_BLOCK % page == 0 and n_past_tokens % MOBA_BLOCK == 0
    assert t <= LANES and n_past_tokens // MOBA_BLOCK <= LANES
    n_blocks = n_past_tokens // MOBA_BLOCK + 1
    rounds = min(MOBA_TOPK, n_blocks)
    ns = SEQ_PAIR if n % SEQ_PAIR == 0 else 1
    tok = pl.BlockSpec((ns, t, aw), lambda i, pt: (i, 0, 0))
    const2 = lambda a: pl.BlockSpec(a.shape, lambda i, pt: (0, 0))
    page_specs = [pl.BlockSpec((None, None, aw, page),
                               lambda i, pt, p=p, sq=sq: (layer, pt[i * ns + sq, p], 0, 0))
                  for sq in range(ns) for p in range(n_pages)]
    grid_spec = pltpu.PrefetchScalarGridSpec(
        num_scalar_prefetch=1,
        grid=(n // ns,),
        in_specs=[tok, tok, tok, const2(bprev), const2(bown), const2(far)] + page_specs + page_specs,
        out_specs=pl.BlockSpec((ns, t, aw), lambda i, pt: (i, 0, 0)),
    )
    return pl.pallas_call(
        functools.partial(_attn_sample_kernel, n_pages, page, rounds),
        grid_spec=grid_spec,
        out_shape=jax.ShapeDtypeStruct((n, t, aw), F32),
        compiler_params=_params(1),
        name="moba_sample",
    )(page_table, q, k_new, v_new, bprev, bown, far, *([cache_kt] * (ns * n_pages)),
      *([cache_vt] * (ns * n_pages)))


def _mixer_out_kernel(x_ref, gt_ref, c_ref, a_ref, sga_ref, sgb_ref, wc_ref, wa_ref, wo_ref, o_ref):
    conv_y = _dot(c_ref[...].astype(BF16), wc_ref[...])
    attn_y = _dot(a_ref[...].astype(BF16), wa_ref[...])
    mixed = (sga_ref[...].astype(F32) * conv_y + sgb_ref[...].astype(F32) * attn_y).astype(BF16)
    o_ref[...] = x_ref[...] + gt_ref[...] * _dot(mixed, wo_ref[...])


def _mixer_out(x, gt, cact, att, sga, sgb, wc, wa, wo, layer, tiles_per_seq):
    nt, d = x.shape
    tm = min(TOKEN_TILE, nt)
    row = lambda i: (i, 0)
    return pl.pallas_call(
        _mixer_out_kernel,
        grid=(nt // tm,),
        in_specs=[pl.BlockSpec((tm, d), row), _mod_spec(gt, tm, tiles_per_seq),
                  pl.BlockSpec((tm, cact.shape[1]), row), pl.BlockSpec((tm, att.shape[1]), row),
                  pl.BlockSpec((tm, d), row), pl.BlockSpec((tm, d), row),
                  _stacked_spec(wc, (layer,)), _stacked_spec(wa, (layer,)),
                  _stacked_spec(wo, (layer,))],
        out_specs=pl.BlockSpec((tm, d), row),
        out_shape=jax.ShapeDtypeStruct((nt, d), F32),
        compiler_params=_params(1),
        name="mixer_out",
    )(x, gt, cact, att, sga, sgb, wc, wa, wo)


def kernel(x_prompt, x_sample, c_prompt, c_sample, cache_k, cache_v, state_conv, page_table, rel_bias, w_ada, b_ada, g_norm, w_ffn_gate, w_ffn_up, w_ffn_down, w_in, conv_w, conv_b, conv_ln_g, conv_ln_b, w_conv_out, w_attn_out, w_o, g_final):
    b, s, d = x_prompt.shape
    n, t, _ = x_sample.shape
    depth = w_ada.shape[0]
    cw = conv_w.shape[2]
    aw = N_HEADS * HEAD_DIM
    d_ff = w_ffn_gate.shape[-1]

    nc = -(-(b + n) // 8) * 8
    c_all = jnp.concatenate([c_prompt, c_sample, jnp.zeros((nc - b - n, d), F32)], axis=0)
    mod = _ada(c_all, w_ada, b_ada)
    assert t <= LANES
    tiles = _bias_tiles(rel_bias, 1.0, LANES)
    tiles_log2 = _bias_tiles(rel_bias, LOG2E, Q_BLOCKS * MOBA_BLOCK)
    far_bias = rel_bias[:, N_BUCKETS - 1]
    far_log2 = far_bias * LOG2E
    far_hi = far_log2.astype(BF16).astype(F32)
    far_parts = jnp.stack([far_hi, far_log2 - far_hi])

    rows = t * N_HEADS
    own_key = N_PAD * MOBA_BLOCK
    bprev_s = tiles[:, own_key - MOBA_BLOCK:own_key, :t].transpose(2, 0, 1).reshape(rows, MOBA_BLOCK)
    bown_s = jnp.pad(tiles[:, own_key:own_key + t, :t].transpose(2, 0, 1).reshape(rows, t),
                     ((0, 0), (0, LANES - t)), constant_values=NEG)
    far_s = jnp.broadcast_to(jnp.tile(far_bias, t)[:, None], (rows, MOBA_BLOCK))

    cache_kt = cache_k.transpose(0, 1, 3, 4, 2).reshape(cache_k.shape[:2] + (aw, cache_k.shape[2]))
    cache_vt = cache_v.transpose(0, 1, 3, 4, 2).reshape(cache_v.shape[:2] + (aw, cache_v.shape[2]))
    state_t = state_conv.transpose(0, 2, 1, 3)
    hist = state_t.shape[1]

    assert d_ff % MXU_DIM == 0
    wg, wu, wd = w_ffn_gate.astype(BF16), w_ffn_up.astype(BF16), w_ffn_down.astype(BF16)
    w_in_b = w_in.astype(BF16)
    w_qkv_t = w_in[:, :, 2 * cw:2 * cw + 3 * aw].transpose(0, 2, 1).astype(BF16)
    wc_b = w_conv_out.astype(BF16)
    wa_b = w_attn_out.astype(BF16)
    wo_b = w_o.astype(BF16)
    gf = g_final.reshape(1, d)

    def ffn(x, mm, l, i, tiles_per_seq, final):
        o = 6 * i
        return _ffn(x, mm[o], mm[o + 1], mm[o + 2], g_norm[l, 2 * i:2 * i + 1], wg, wu, wd, (l, i),
                    tiles_per_seq, gf if final else None)

    tm_p = min(TOKEN_TILE, s)
    assert s % tm_p == 0 and s % MOBA_BLOCK == 0
    tps = s // tm_p
    x = x_prompt.reshape(b * s, d)
    k_stack = jnp.zeros((depth, b, aw, s), F32)
    v_stack = jnp.zeros((depth, b, aw, s), F32)
    cp = []
    for l in range(depth):
        mm = [mod[l, i, :b][:, None, :] for i in range(N_MOD)]
        x = ffn(x, mm, l, 0, tps, False)
        u, q_t, k_stack, v_stack, sga, sgb = _mixer_in(x, mm[3], mm[4], g_norm[l, 1:2], w_in_b, l, cw,
                                                       aw, tps, w_qkv_t, (k_stack, v_stack))
        u3 = u.reshape(b, s, cw)
        cact = _conv_prompt(u3, conv_w[l], conv_b[l], conv_ln_g[l], conv_ln_b[l])
        att = _attn_prompt(l, q_t, k_stack, v_stack, tiles_log2, far_parts)
        x = _mixer_out(x, mm[5], cact.reshape(b * s, cw), att.reshape(b * s, aw), sga, sgb,
                       wc_b, wa_b, wo_b, l, tps)
        x = ffn(x, mm, l, 1, tps, l == depth - 1)
        cp.append(u3[:, s - hist:])
    y_p = x.reshape(b, s, d)
    to_cache = lambda z: z.reshape(depth, b, N_HEADS, HEAD_DIM, s).transpose(0, 1, 4, 2, 3)
    k_p, v_p, c_p = to_cache(k_stack), to_cache(v_stack), jnp.stack(cp)

    x = x_sample.reshape(n * t, d)
    ks, vs, cs = [], [], []
    for l in range(depth):
        mm = [jnp.repeat(mod[l, i, b:b + n], t, axis=0) for i in range(N_MOD)]
        x = ffn(x, mm, l, 0, 1, False)
        u, q, k, v, sga, sgb = _mixer_in(x, mm[3], mm[4], g_norm[l, 1:2], w_in_b, l, cw, aw, 1)
        u_t = u.reshape(n, t, cw).transpose(1, 0, 2)
        cact_t, new_state = _conv_sample(l, state_t, u_t, conv_w[l], conv_b[l], conv_ln_g[l],
                                         conv_ln_b[l])
        att = _attn_sample(l, q.reshape(n, t, aw), k.reshape(n, t, aw), v.reshape(n, t, aw),
                           cache_kt, cache_vt, page_table, bprev_s, bown_s, far_s)
        x = _mixer_out(x, mm[5], cact_t.transpose(1, 0, 2).reshape(n * t, cw), att.reshape(n * t, aw),
                       sga, sgb, wc_b, wa_b, wo_b, l, 1)
        x = ffn(x, mm, l, 1, 1, l == depth - 1)
        ks.append(k.reshape(n, t, N_HEADS, HEAD_DIM))
        vs.append(v.reshape(n, t, N_HEADS, HEAD_DIM))
        cs.append(new_state)
    y_s = x.reshape(n, t, d)
    return (y_p, y_s, k_p, v_p, c_p, jnp.stack(ks), jnp.stack(vs),
            jnp.stack(cs).transpose(0, 2, 1, 3))
```
